```python
import math
import jax, jax.numpy as jnp
from jax import lax
import numpy as np

D_MODEL = 1024
BATCH = 1
SEQ = 16384
DEPTH = 1

N_HEADS = 8
HEAD_DIM = 64
N_KV = 2
HPG = N_HEADS // N_KV
ATTN_WIDTH = N_HEADS * HEAD_DIM
SSM_GROUP = 16
SSM_GROUPS = 32
SSM_WIDTH = SSM_GROUP * SSM_GROUPS
SSM_STATE = 64
MIX_WIDTH = ATTN_WIDTH + SSM_WIDTH
KV_WIDTH = N_KV * HEAD_DIM
IN_WIDTH = ATTN_WIDTH + 6 * KV_WIDTH + 3 * N_HEADS + SSM_WIDTH
CMP_LEN = 32
CMP_STRIDE = 16
CMP_HIDDEN = 128
SEL_BLOCK = 64
N_SEL = 16
WINDOW = 512
Q_BLOCK = 128
ROPE_THETA = 500000.0
ROPE_DIM = HEAD_DIM // 4
PEER_HEADS = 8
PEER_NKEYS = 128
PEER_EXPERTS = PEER_NKEYS * PEER_NKEYS
PEER_QDIM = 256
PEER_TOPK = 16
PEER_CHUNK = 128
NORM_EPS = 1e-6
NEG_INF = -1e30
FORCE_SCORE = 1e9

kernel_name = "hymba_nsa_s5_peer_block"


def rms_norm(x, g):
    xf = x.astype(jnp.float32)
    y = xf * lax.rsqrt(jnp.mean(xf * xf, axis=-1, keepdims=True) + NORM_EPS)
    return (y * g.astype(jnp.float32)).astype(x.dtype)


def rope(x, pos):
    half = ROPE_DIM // 2
    inv = ROPE_THETA ** (-jnp.arange(half, dtype=jnp.float32) * 2.0 / ROPE_DIM)
    ang = pos.astype(jnp.float32)[:, None] * inv[None, :]
    cos = jnp.cos(ang)[None, :, None, :]
    sin = jnp.sin(ang)[None, :, None, :]
    xr = x[..., :ROPE_DIM].astype(jnp.float32)
    x1, x2 = xr[..., :half], xr[..., half:]
    rot = jnp.concatenate([x1 * cos - x2 * sin, x2 * cos + x1 * sin], axis=-1).astype(x.dtype)
    return jnp.concatenate([rot, x[..., ROPE_DIM:]], axis=-1)


def masked_softmax(s, mask):
    s = jnp.where(mask, s, NEG_INF)
    m = jnp.max(s, axis=-1, keepdims=True)
    e = jnp.where(mask, jnp.exp(s - m), 0.0)
    return e / jnp.maximum(jnp.sum(e, axis=-1, keepdims=True), 1e-30)


def compress(kv, pe, w1, w2):
    b, s = kv.shape[0], kv.shape[1]
    n_cmp = (s - CMP_LEN) // CMP_STRIDE + 1
    idx = jnp.arange(n_cmp)[:, None] * CMP_STRIDE + jnp.arange(CMP_LEN)[None, :]
    blk = kv[:, idx] + pe[None, None, :, None, :]
    blk = blk.transpose(0, 1, 3, 2, 4).reshape(b, n_cmp, N_KV, CMP_LEN * HEAD_DIM)
    return jax.nn.gelu(blk @ w1) @ w2


def nsa_attention(q, kc_raw, vc_raw, ks, vs, kw, vw, gates, pe_k, w1k, w2k, pe_v, w1v, w2v):
    b, s = q.shape[0], q.shape[1]
    pos = jnp.arange(s)
    n_blk = s // Q_BLOCK
    n_selb = s // SEL_BLOCK
    n_sel = min(N_SEL, n_selb)
    q = rope(q, pos) * (HEAD_DIM ** -0.5)
    kc = compress(kc_raw, pe_k, w1k, w2k)
    vc = compress(vc_raw, pe_v, w1v, w2v)
    n_cmp = kc.shape[1]
    cmp_start = jnp.arange(n_cmp) * CMP_STRIDE
    cmp_end = cmp_start + CMP_LEN - 1
    kc = rope(kc, cmp_end).transpose(0, 2, 1, 3)
    vc = vc.transpose(0, 2, 1, 3)
    sel_ids = jnp.arange(n_selb)
    sel_start = sel_ids * SEL_BLOCK
    overlap = ((cmp_start[:, None] < sel_start[None, :] + SEL_BLOCK)
               & (cmp_start[:, None] + CMP_LEN > sel_start[None, :])).astype(jnp.float32)
    ks_blk = rope(ks, pos).reshape(b, n_selb, SEL_BLOCK, N_KV, HEAD_DIM).transpose(0, 3, 1, 2, 4)
    vs_blk = vs.reshape(b, n_selb, SEL_BLOCK, N_KV, HEAD_DIM).transpose(0, 3, 1, 2, 4)
    kw_pad = jnp.pad(rope(kw, pos), ((0, 0), (WINDOW, 0), (0, 0), (0, 0))).transpose(0, 2, 1, 3)
    vw_pad = jnp.pad(vw, ((0, 0), (WINDOW, 0), (0, 0), (0, 0))).transpose(0, 2, 1, 3)
    q_blocks = q.reshape(b, n_blk, Q_BLOCK, N_KV, HPG, HEAD_DIM).transpose(1, 0, 3, 4, 2, 5)
    g_blocks = jax.nn.sigmoid(gates.astype(jnp.float32)).reshape(
        b, n_blk, Q_BLOCK, N_KV, HPG, 3).transpose(1, 0, 3, 4, 2, 5)
    b_ix = jnp.arange(b)[:, None, None, None]
    g_ix = jnp.arange(N_KV)[None, :, None, None]

    def block(args):
        blk, qb, gb = args
        t = blk * Q_BLOCK + jnp.arange(Q_BLOCK)
        s_c = jnp.einsum('bghqd,bgnd->bghqn', qb, kc).astype(jnp.float32)
        p_c = masked_softmax(s_c, cmp_end[None, :] <= t[:, None])
        o_c = jnp.einsum('bghqn,bgnd->bghqd', p_c.astype(vc.dtype), vc)
        imp = jnp.einsum('bghqn,nm->bgqm', p_c, overlap)
        cur = (t // SEL_BLOCK)[:, None]
        valid = sel_start[None, :] <= t[:, None]
        forced = (sel_ids[None, :] == 0) | (sel_ids[None, :] == cur) | (sel_ids[None, :] == cur - 1)
        imp = jnp.where(forced, FORCE_SCORE, jnp.where(valid, imp, NEG_INF))
        _, sidx = lax.top_k(imp, n_sel)
        k_sel = ks_blk[b_ix, g_ix, sidx]
        v_sel = vs_blk[b_ix, g_ix, sidx]
        s_s = jnp.einsum('bghqd,bgqnkd->bghqnk', qb, k_sel).astype(jnp.float32)
        kpos = sidx[..., None] * SEL_BLOCK + jnp.arange(SEL_BLOCK)
        m_s = (kpos <= t[:, None, None])[:, :, None]
        shp = s_s.shape
        p_s = masked_softmax(s_s.reshape(shp[0], shp[1], shp[2], shp[3], -1),
                             m_s.reshape(shp[0], shp[1], 1, shp[3], -1)).reshape(shp)
        o_s = jnp.einsum('bghqnk,bgqnkd->bghqd', p_s.astype(v_sel.dtype), v_sel)
        start = blk * Q_BLOCK
        k_w = lax.dynamic_slice_in_dim(kw_pad, start, Q_BLOCK + WINDOW, axis=2)
        v_w = lax.dynamic_slice_in_dim(vw_pad, start, Q_BLOCK + WINDOW, axis=2)
        kpos_w = start - WINDOW + jnp.arange(Q_BLOCK + WINDOW)
        m_w = ((kpos_w[None, :] >= 0) & (kpos_w[None, :] <= t[:, None])
               & (t[:, None] - kpos_w[None, :] < WINDOW))
        s_w = jnp.einsum('bghqd,bgkd->bghqk', qb, k_w).astype(jnp.float32)
        p_w = masked_softmax(s_w, m_w)
        o_w = jnp.einsum('bghqk,bgkd->bghqd', p_w.astype(v_w.dtype), v_w)
        gb = gb.astype(qb.dtype)
        return gb[..., 0:1] * o_c + gb[..., 1:2] * o_s + gb[..., 2:3] * o_w

    out = lax.map(block, (jnp.arange(n_blk), q_blocks, g_blocks))
    return out.transpose(1, 0, 4, 2, 3, 5).reshape(b, s, ATTN_WIDTH)


def s5_ssm(u, lam_re, lam_im, log_dt, b_re, b_im, c_re, c_im, d_skip, w_glu):
    b, s = u.shape[0], u.shape[1]
    f32 = jnp.float32
    lam = lax.complex(lam_re.astype(f32), lam_im.astype(f32))
    dt = jnp.exp(log_dt.astype(f32))[:, None]
    lam_bar = jnp.exp(lam * dt)
    bmat = lax.complex(b_re.astype(f32), b_im.astype(f32))
    b_bar = ((lam_bar - 1.0) / lam)[..., None] * bmat
    cmat = lax.complex(c_re.astype(f32), c_im.astype(f32))
    uf = u.astype(f32)
    ug = uf.reshape(b, s, SSM_GROUPS, SSM_GROUP).astype(jnp.complex64)
    bu = jnp.einsum('gpc,bsgc->bsgp', b_bar, ug)
    a = jnp.broadcast_to(lam_bar, bu.shape)

    def combine(e1, e2):
        a1, b1 = e1
        a2, b2 = e2
        return (a1 * a2, a2 * b1 + b2)

    _, states = lax.associative_scan(combine, (a, bu), axis=1)
    y = jnp.real(jnp.einsum('gcp,bsgp->bsgc', cmat, states)).reshape(b, s, SSM_WIDTH)
    y = jax.nn.gelu(y + d_skip.astype(f32) * uf).astype(u.dtype)
    ab = y @ w_glu
    return ab[..., :SSM_WIDTH] * jax.nn.sigmoid(ab[..., SSM_WIDTH:])


def peer_ffn(x, w_q, sub1, sub2, u_tab, v_tab):
    b, s, d = x.shape
    n_ch = s // PEER_CHUNK
    half = PEER_QDIM // 2
    xc = x.reshape(b, n_ch, PEER_CHUNK, d).transpose(1, 0, 2, 3)

    def chunk(xb):
        q = (xb @ w_q).reshape(b, PEER_CHUNK, PEER_HEADS, PEER_QDIM)
        s1 = jnp.einsum('bchd,kd->bchk', q[..., :half], sub1).astype(jnp.float32)
        s2 = jnp.einsum('bchd,kd->bchk', q[..., half:], sub2).astype(jnp.float32)
        v1, i1 = lax.top_k(s1, PEER_TOPK)
        v2, i2 = lax.top_k(s2, PEER_TOPK)
        cand = (v1[..., :, None] + v2[..., None, :]).reshape(b, PEER_CHUNK, PEER_HEADS, PEER_TOPK * PEER_TOPK)
        cidx = (i1[..., :, None] * PEER_NKEYS + i2[..., None, :]).reshape(b, PEER_CHUNK, PEER_HEADS, PEER_TOPK * PEER_TOPK)
        top, sel = lax.top_k(cand, PEER_TOPK)
        eidx = jnp.take_along_axis(cidx, sel, axis=-1)
        gate = jax.nn.softmax(top, axis=-1)
        u_sel = u_tab[eidx]
        v_sel = v_tab[eidx]
        hid = jnp.einsum('bcd,bchkd->bchk', xb, u_sel).astype(jnp.float32)
        act = (jax.nn.gelu(hid) * gate).astype(xb.dtype)
        return jnp.einsum('bchk,bchkd->bcd', act, v_sel)

    out = lax.map(chunk, xc)
    return out.transpose(1, 0, 2, 3).reshape(b, s, d)


def setup_inputs(seed: int = 0) -> dict:
    key = jax.random.key(seed)
    ks = jax.random.split(key, 25)
    f32 = jnp.float32

    def nrm(k, shape, scale):
        return jax.random.normal(k, shape, f32) * scale

    return {
        'x': nrm(ks[0], (BATCH, SEQ, D_MODEL), 1.0),
        'attn_norm': 1.0 + nrm(ks[1], (DEPTH, D_MODEL), 0.01),
        'w_in': nrm(ks[2], (DEPTH, D_MODEL, IN_WIDTH), D_MODEL ** -0.5),
        'cmp_k_pe': nrm(ks[3], (DEPTH, CMP_LEN, HEAD_DIM), 0.1),
        'cmp_k_w1': nrm(ks[4], (DEPTH, CMP_LEN * HEAD_DIM, CMP_HIDDEN), (CMP_LEN * HEAD_DIM) ** -0.5),
        'cmp_k_w2': nrm(ks[5], (DEPTH, CMP_HIDDEN, HEAD_DIM), CMP_HIDDEN ** -0.5),
        'cmp_v_pe': nrm(ks[6], (DEPTH, CMP_LEN, HEAD_DIM), 0.1),
        'cmp_v_w1': nrm(ks[7], (DEPTH, CMP_LEN * HEAD_DIM, CMP_HIDDEN), (CMP_LEN * HEAD_DIM) ** -0.5),
        'cmp_v_w2': nrm(ks[8], (DEPTH, CMP_HIDDEN, HEAD_DIM), CMP_HIDDEN ** -0.5),
        'ssm_lam_re': -0.5 + nrm(ks[9], (DEPTH, SSM_GROUPS, SSM_STATE), 0.01),
        'ssm_lam_im': jnp.broadcast_to(math.pi * jnp.arange(SSM_STATE, dtype=f32), (DEPTH, SSM_GROUPS, SSM_STATE)),
        'ssm_log_dt': jax.random.uniform(ks[10], (DEPTH, SSM_GROUPS), f32, math.log(1e-3), math.log(1e-1)),
        'ssm_b_re': nrm(ks[11], (DEPTH, SSM_GROUPS, SSM_STATE, SSM_GROUP), (2 * SSM_GROUP) ** -0.5),
        'ssm_b_im': nrm(ks[12], (DEPTH, SSM_GROUPS, SSM_STATE, SSM_GROUP), (2 * SSM_GROUP) ** -0.5),
        'ssm_c_re': nrm(ks[13], (DEPTH, SSM_GROUPS, SSM_GROUP, SSM_STATE), 0.5),
        'ssm_c_im': nrm(ks[14], (DEPTH, SSM_GROUPS, SSM_GROUP, SSM_STATE), 0.5),
        'ssm_d': nrm(ks[15], (DEPTH, SSM_WIDTH), 1.0),
        'ssm_w_glu': nrm(ks[16], (DEPTH, SSM_WIDTH, 2 * SSM_WIDTH), SSM_WIDTH ** -0.5),
        'w_out': nrm(ks[17], (DEPTH, MIX_WIDTH, D_MODEL), MIX_WIDTH ** -0.5),
        'ffn_norm': 1.0 + nrm(ks[18], (DEPTH, D_MODEL), 0.01),
        'peer_w_q': nrm(ks[19], (DEPTH, D_MODEL, PEER_HEADS * PEER_QDIM), D_MODEL ** -0.5),
        'peer_subkeys_1': nrm(ks[20], (DEPTH, PEER_NKEYS, PEER_QDIM // 2), (PEER_QDIM // 2) ** -0.5),
        'peer_subkeys_2': nrm(ks[21], (DEPTH, PEER_NKEYS, PEER_QDIM // 2), (PEER_QDIM // 2) ** -0.5),
        'peer_u': nrm(ks[22], (DEPTH, PEER_EXPERTS, D_MODEL), D_MODEL ** -0.5),
        'peer_v': nrm(ks[23], (DEPTH, PEER_EXPERTS, D_MODEL), PEER_HEADS ** -0.5),
        'final_norm': 1.0 + nrm(ks[24], (D_MODEL,), 0.01),
    }


def reference(x, attn_norm, w_in, cmp_k_pe, cmp_k_w1, cmp_k_w2, cmp_v_pe, cmp_v_w1, cmp_v_w2,
              ssm_lam_re, ssm_lam_im, ssm_log_dt, ssm_b_re, ssm_b_im, ssm_c_re, ssm_c_im, ssm_d,
              ssm_w_glu, w_out, ffn_norm, peer_w_q, peer_subkeys_1, peer_subkeys_2, peer_u, peer_v,
              final_norm):
    b, s, _ = x.shape
    sizes = [ATTN_WIDTH] + [KV_WIDTH] * 6 + [3 * N_HEADS]
    cuts = [int(c) for c in np.cumsum(sizes)]
    h = x
    for l in range(DEPTH):
        z = rms_norm(h, attn_norm[l])
        proj = z @ w_in[l]
        q, kc, vc, ksl, vsl, kwn, vwn, gates, u = jnp.split(proj, cuts, axis=-1)
        kv_shape = (b, s, N_KV, HEAD_DIM)
        a_out = nsa_attention(q.reshape(b, s, N_HEADS, HEAD_DIM),
                              kc.reshape(kv_shape), vc.reshape(kv_shape),
                              ksl.reshape(kv_shape), vsl.reshape(kv_shape),
                              kwn.reshape(kv_shape), vwn.reshape(kv_shape), gates,
                              cmp_k_pe[l], cmp_k_w1[l], cmp_k_w2[l],
                              cmp_v_pe[l], cmp_v_w1[l], cmp_v_w2[l])
        s_out = s5_ssm(u, ssm_lam_re[l], ssm_lam_im[l], ssm_log_dt[l], ssm_b_re[l], ssm_b_im[l],
                       ssm_c_re[l], ssm_c_im[l], ssm_d[l], ssm_w_glu[l])
        h = h + jnp.concatenate([a_out, s_out], axis=-1) @ w_out[l]
        h = h + peer_ffn(rms_norm(h, ffn_norm[l]), peer_w_q[l], peer_subkeys_1[l],
                         peer_subkeys_2[l], peer_u[l], peer_v[l])
    return rms_norm(h, final_norm)
```

```python
import math
import jax, jax.numpy as jnp
from jax import lax
import numpy as np
from jax.experimental import pallas as pl

D_MODEL = 1024
DEPTH = 1
N_HEADS = 8
HEAD_DIM = 64
N_KV = 2
HPG = N_HEADS // N_KV
ATTN_WIDTH = N_HEADS * HEAD_DIM
SSM_GROUP = 16
SSM_GROUPS = 32
SSM_WIDTH = SSM_GROUP * SSM_GROUPS
SSM_STATE = 64
MIX_WIDTH = ATTN_WIDTH + SSM_WIDTH
KV_WIDTH = N_KV * HEAD_DIM
CMP_LEN = 32
CMP_STRIDE = 16
CMP_HIDDEN = 128
SEL_BLOCK = 64
N_SEL = 16
WINDOW = 512
Q_BLOCK = 128
ROPE_THETA = 500000.0
ROPE_DIM = HEAD_DIM // 4
PEER_HEADS = 8
PEER_NKEYS = 128
PEER_QDIM = 256
PEER_TOPK = 16
PEER_CHUNK = 128
NORM_EPS = 1e-6
NEG_INF = -1e30
FORCE_SCORE = 1e9


def rms_norm(x, g):
    xf = x.astype(jnp.float32)
    y = xf * lax.rsqrt(jnp.mean(xf * xf, axis=-1, keepdims=True) + NORM_EPS)
    return (y * g.astype(jnp.float32)).astype(x.dtype)


def rope(x, pos):
    half = ROPE_DIM // 2
    inv = ROPE_THETA ** (-jnp.arange(half, dtype=jnp.float32) * 2.0 / ROPE_DIM)
    ang = pos.astype(jnp.float32)[:, None] * inv[None, :]
    cos = jnp.cos(ang)[None, :, None, :]
    sin = jnp.sin(ang)[None, :, None, :]
    xr = x[..., :ROPE_DIM].astype(jnp.float32)
    x1, x2 = xr[..., :half], xr[..., half:]
    rot = jnp.concatenate([x1 * cos - x2 * sin, x2 * cos + x1 * sin], axis=-1).astype(x.dtype)
    return jnp.concatenate([rot, x[..., ROPE_DIM:]], axis=-1)


def masked_softmax(s, mask):
    s = jnp.where(mask, s, NEG_INF)
    m = jnp.max(s, axis=-1, keepdims=True)
    e = jnp.where(mask, jnp.exp(s - m), 0.0)
    return e / jnp.maximum(jnp.sum(e, axis=-1, keepdims=True), 1e-30)


def compress(kv, pe, w1, w2):
    b, s = kv.shape[0], kv.shape[1]
    n_cmp = (s - CMP_LEN) // CMP_STRIDE + 1
    idx = jnp.arange(n_cmp)[:, None] * CMP_STRIDE + jnp.arange(CMP_LEN)[None, :]
    blk = kv[:, idx] + pe[None, None, :, None, :]
    blk = blk.transpose(0, 1, 3, 2, 4).reshape(b, n_cmp, N_KV, CMP_LEN * HEAD_DIM)
    return jax.nn.gelu(blk @ w1) @ w2


def nsa_attention(q, kc_raw, vc_raw, ks, vs, kw, vw, gates, pe_k, w1k, w2k, pe_v, w1v, w2v):
    b, s = q.shape[0], q.shape[1]
    pos = jnp.arange(s)
    n_blk = s // Q_BLOCK
    n_selb = s // SEL_BLOCK
    n_sel = min(N_SEL, n_selb)
    q = rope(q, pos) * (HEAD_DIM ** -0.5)
    kc = compress(kc_raw, pe_k, w1k, w2k)
    vc = compress(vc_raw, pe_v, w1v, w2v)
    n_cmp = kc.shape[1]
    cmp_start = jnp.arange(n_cmp) * CMP_STRIDE
    cmp_end = cmp_start + CMP_LEN - 1
    kc = rope(kc, cmp_end).transpose(0, 2, 1, 3)
    vc = vc.transpose(0, 2, 1, 3)
    sel_ids = jnp.arange(n_selb)
    sel_start = sel_ids * SEL_BLOCK
    overlap = ((cmp_start[:, None] < sel_start[None, :] + SEL_BLOCK)
               & (cmp_start[:, None] + CMP_LEN > sel_start[None, :])).astype(jnp.float32)
    ks_blk = rope(ks, pos).reshape(b, n_selb, SEL_BLOCK, N_KV, HEAD_DIM).transpose(0, 3, 1, 2, 4)
    vs_blk = vs.reshape(b, n_selb, SEL_BLOCK, N_KV, HEAD_DIM).transpose(0, 3, 1, 2, 4)
    kw_pad = jnp.pad(rope(kw, pos), ((0, 0), (WINDOW, 0), (0, 0), (0, 0))).transpose(0, 2, 1, 3)
    vw_pad = jnp.pad(vw, ((0, 0), (WINDOW, 0), (0, 0), (0, 0))).transpose(0, 2, 1, 3)
    q_blocks = q.reshape(b, n_blk, Q_BLOCK, N_KV, HPG, HEAD_DIM).transpose(1, 0, 3, 4, 2, 5)
    g_blocks = jax.nn.sigmoid(gates.astype(jnp.float32)).reshape(
        b, n_blk, Q_BLOCK, N_KV, HPG, 3).transpose(1, 0, 3, 4, 2, 5)
    b_ix = jnp.arange(b)[:, None, None, None]
    g_ix = jnp.arange(N_KV)[None, :, None, None]

    def block(args):
        blk, qb, gb = args
        t = blk * Q_BLOCK + jnp.arange(Q_BLOCK)
        s_c = jnp.einsum('bghqd,bgnd->bghqn', qb, kc).astype(jnp.float32)
        p_c = masked_softmax(s_c, cmp_end[None, :] <= t[:, None])
        o_c = jnp.einsum('bghqn,bgnd->bghqd', p_c.astype(vc.dtype), vc)
        imp = jnp.einsum('bghqn,nm->bgqm', p_c, overlap)
        cur = (t // SEL_BLOCK)[:, None]
        valid = sel_start[None, :] <= t[:, None]
        forced = (sel_ids[None, :] == 0) | (sel_ids[None, :] == cur) | (sel_ids[None, :] == cur - 1)
        imp = jnp.where(forced, FORCE_SCORE, jnp.where(valid, imp, NEG_INF))
        _, sidx = lax.top_k(imp, n_sel)
        k_sel = ks_blk[b_ix, g_ix, sidx]
        v_sel = vs_blk[b_ix, g_ix, sidx]
        s_s = jnp.einsum('bghqd,bgqnkd->bghqnk', qb, k_sel).astype(jnp.float32)
        kpos = sidx[..., None] * SEL_BLOCK + jnp.arange(SEL_BLOCK)
        m_s = (kpos <= t[:, None, None])[:, :, None]
        shp = s_s.shape
        p_s = masked_softmax(s_s.reshape(shp[0], shp[1], shp[2], shp[3], -1),
                             m_s.reshape(shp[0], shp[1], 1, shp[3], -1)).reshape(shp)
        o_s = jnp.einsum('bghqnk,bgqnkd->bghqd', p_s.astype(v_sel.dtype), v_sel)
        start = blk * Q_BLOCK
        k_w = lax.dynamic_slice_in_dim(kw_pad, start, Q_BLOCK + WINDOW, axis=2)
        v_w = lax.dynamic_slice_in_dim(vw_pad, start, Q_BLOCK + WINDOW, axis=2)
        kpos_w = start - WINDOW + jnp.arange(Q_BLOCK + WINDOW)
        m_w = ((kpos_w[None, :] >= 0) & (kpos_w[None, :] <= t[:, None])
               & (t[:, None] - kpos_w[None, :] < WINDOW))
        s_w = jnp.einsum('bghqd,bgkd->bghqk', qb, k_w).astype(jnp.float32)
        p_w = masked_softmax(s_w, m_w)
        o_w = jnp.einsum('bghqk,bgkd->bghqd', p_w.astype(v_w.dtype), v_w)
        gb = gb.astype(qb.dtype)
        return gb[..., 0:1] * o_c + gb[..., 1:2] * o_s + gb[..., 2:3] * o_w

    out = lax.map(block, (jnp.arange(n_blk), q_blocks, g_blocks))
    return out.transpose(1, 0, 4, 2, 3, 5).reshape(b, s, ATTN_WIDTH)


def s5_ssm(u, lam_re, lam_im, log_dt, b_re, b_im, c_re, c_im, d_skip, w_glu):
    b, s = u.shape[0], u.shape[1]
    f32 = jnp.float32
    lam = lax.complex(lam_re.astype(f32), lam_im.astype(f32))
    dt = jnp.exp(log_dt.astype(f32))[:, None]
    lam_bar = jnp.exp(lam * dt)
    bmat = lax.complex(b_re.astype(f32), b_im.astype(f32))
    b_bar = ((lam_bar - 1.0) / lam)[..., None] * bmat
    cmat = lax.complex(c_re.astype(f32), c_im.astype(f32))
    uf = u.astype(f32)
    ug = uf.reshape(b, s, SSM_GROUPS, SSM_GROUP).astype(jnp.complex64)
    bu = jnp.einsum('gpc,bsgc->bsgp', b_bar, ug)
    a = jnp.broadcast_to(lam_bar, bu.shape)

    def combine(e1, e2):
        a1, b1 = e1
        a2, b2 = e2
        return (a1 * a2, a2 * b1 + b2)

    _, states = lax.associative_scan(combine, (a, bu), axis=1)
    y = jnp.real(jnp.einsum('gcp,bsgp->bsgc', cmat, states)).reshape(b, s, SSM_WIDTH)
    y = jax.nn.gelu(y + d_skip.astype(f32) * uf).astype(u.dtype)
    ab = y @ w_glu
    return ab[..., :SSM_WIDTH] * jax.nn.sigmoid(ab[..., SSM_WIDTH:])


def peer_ffn(x, w_q, sub1, sub2, u_tab, v_tab):
    b, s, d = x.shape
    n_ch = s // PEER_CHUNK
    half = PEER_QDIM // 2
    xc = x.reshape(b, n_ch, PEER_CHUNK, d).transpose(1, 0, 2, 3)

    def chunk(xb):
        q = (xb @ w_q).reshape(b, PEER_CHUNK, PEER_HEADS, PEER_QDIM)
        s1 = jnp.einsum('bchd,kd->bchk', q[..., :half], sub1).astype(jnp.float32)
        s2 = jnp.einsum('bchd,kd->bchk', q[..., half:], sub2).astype(jnp.float32)
        v1, i1 = lax.top_k(s1, PEER_TOPK)
        v2, i2 = lax.top_k(s2, PEER_TOPK)
        cand = (v1[..., :, None] + v2[..., None, :]).reshape(b, PEER_CHUNK, PEER_HEADS, PEER_TOPK * PEER_TOPK)
        cidx = (i1[..., :, None] * PEER_NKEYS + i2[..., None, :]).reshape(b, PEER_CHUNK, PEER_HEADS, PEER_TOPK * PEER_TOPK)
        top, sel = lax.top_k(cand, PEER_TOPK)
        eidx = jnp.take_along_axis(cidx, sel, axis=-1)
        gate = jax.nn.softmax(top, axis=-1)
        u_sel = u_tab[eidx]
        v_sel = v_tab[eidx]
        hid = jnp.einsum('bcd,bchkd->bchk', xb, u_sel).astype(jnp.float32)
        act = (jax.nn.gelu(hid) * gate).astype(xb.dtype)
        return jnp.einsum('bchk,bchkd->bcd', act, v_sel)

    out = lax.map(chunk, xc)
    return out.transpose(1, 0, 2, 3).reshape(b, s, d)


def _final_norm_kernel(h_ref, g_ref, o_ref):
    xf = h_ref[...]
    y = xf * lax.rsqrt(jnp.mean(xf * xf, axis=-1, keepdims=True) + NORM_EPS)
    o_ref[...] = y * g_ref[...]


def _final_norm_pallas(h, g):
    b, s, d = h.shape
    tb = 512
    out = pl.pallas_call(
        _final_norm_kernel,
        grid=(b * s // tb,),
        in_specs=[pl.BlockSpec((tb, d), lambda i: (i, 0)), pl.BlockSpec((1, d), lambda i: (0, 0))],
        out_specs=pl.BlockSpec((tb, d), lambda i: (i, 0)),
        out_shape=jax.ShapeDtypeStruct((b * s, d), h.dtype),
        name="final_norm",
    )(h.reshape(b * s, d), g.reshape(1, d))
    return out.reshape(b, s, d)


def kernel(x, attn_norm, w_in, cmp_k_pe, cmp_k_w1, cmp_k_w2, cmp_v_pe, cmp_v_w1, cmp_v_w2, ssm_lam_re, ssm_lam_im, ssm_log_dt, ssm_b_re, ssm_b_im, ssm_c_re, ssm_c_im, ssm_d, ssm_w_glu, w_out, ffn_norm, peer_w_q, peer_subkeys_1, peer_subkeys_2, peer_u, peer_v, final_norm):
    b, s, _ = x.shape
    sizes = [ATTN_WIDTH] + [KV_WIDTH] * 6 + [3 * N_HEADS]
    cuts = [int(c) for c in np.cumsum(sizes)]
    h = x
    for l in range(DEPTH):
        z = rms_norm(h, attn_norm[l])
        proj = z @ w_in[l]
        q, kc, vc, ksl, vsl, kwn, vwn, gates, u = jnp.split(proj, cuts, axis=-1)
        kv_shape = (b, s, N_KV, HEAD_DIM)
        a_out = nsa_attention(q.reshape(b, s, N_HEADS, HEAD_DIM),
                              kc.reshape(kv_shape), vc.reshape(kv_shape),
                              ksl.reshape(kv_shape), vsl.reshape(kv_shape),
                              kwn.reshape(kv_shape), vwn.reshape(kv_shape), gates,
                              cmp_k_pe[l], cmp_k_w1[l], cmp_k_w2[l],
                              cmp_v_pe[l], cmp_v_w1[l], cmp_v_w2[l])
        s_out = s5_ssm(u, ssm_lam_re[l], ssm_lam_im[l], ssm_log_dt[l], ssm_b_re[l], ssm_b_im[l],
                       ssm_c_re[l], ssm_c_im[l], ssm_d[l], ssm_w_glu[l])
        h = h + jnp.concatenate([a_out, s_out], axis=-1) @ w_out[l]
        h = h + peer_ffn(rms_norm(h, ffn_norm[l]), peer_w_q[l], peer_subkeys_1[l],
                         peer_subkeys_2[l], peer_u[l], peer_v[l])
    return _final_norm_pallas(h, final_norm)
```

```python
import functools
import math
import jax, jax.numpy as jnp
from jax import lax
import numpy as np
from jax.experimental import pallas as pl
from jax.experimental.pallas import tpu as pltpu

D_MODEL = 1024
DEPTH = 1
N_HEADS = 8
HEAD_DIM = 64
N_KV = 2
HPG = N_HEADS // N_KV
ATTN_WIDTH = N_HEADS * HEAD_DIM
SSM_GROUP = 16
SSM_GROUPS = 32
SSM_WIDTH = SSM_GROUP * SSM_GROUPS
SSM_STATE = 64
MIX_WIDTH = ATTN_WIDTH + SSM_WIDTH
KV_WIDTH = N_KV * HEAD_DIM
CMP_LEN = 32
CMP_STRIDE = 16
CMP_HIDDEN = 128
SEL_BLOCK = 64
N_SEL = 16
WINDOW = 512
Q_BLOCK = 128
ROPE_THETA = 500000.0
ROPE_DIM = HEAD_DIM // 4
PEER_HEADS = 8
PEER_NKEYS = 128
PEER_QDIM = 256
PEER_TOPK = 16
PEER_CHUNK = 128
NORM_EPS = 1e-6
NEG_INF = -1e30
FORCE_SCORE = 1e9


def rms_norm(x, g):
    xf = x.astype(jnp.float32)
    y = xf * lax.rsqrt(jnp.mean(xf * xf, axis=-1, keepdims=True) + NORM_EPS)
    return (y * g.astype(jnp.float32)).astype(x.dtype)


def rope(x, pos):
    half = ROPE_DIM // 2
    inv = ROPE_THETA ** (-jnp.arange(half, dtype=jnp.float32) * 2.0 / ROPE_DIM)
    ang = pos.astype(jnp.float32)[:, None] * inv[None, :]
    cos = jnp.cos(ang)[None, :, None, :]
    sin = jnp.sin(ang)[None, :, None, :]
    xr = x[..., :ROPE_DIM].astype(jnp.float32)
    x1, x2 = xr[..., :half], xr[..., half:]
    rot = jnp.concatenate([x1 * cos - x2 * sin, x2 * cos + x1 * sin], axis=-1).astype(x.dtype)
    return jnp.concatenate([rot, x[..., ROPE_DIM:]], axis=-1)


def masked_softmax(s, mask):
    s = jnp.where(mask, s, NEG_INF)
    m = jnp.max(s, axis=-1, keepdims=True)
    e = jnp.where(mask, jnp.exp(s - m), 0.0)
    return e / jnp.maximum(jnp.sum(e, axis=-1, keepdims=True), 1e-30)


def compress(kv, pe, w1, w2):
    b, s = kv.shape[0], kv.shape[1]
    n_cmp = (s - CMP_LEN) // CMP_STRIDE + 1
    idx = jnp.arange(n_cmp)[:, None] * CMP_STRIDE + jnp.arange(CMP_LEN)[None, :]
    blk = kv[:, idx] + pe[None, None, :, None, :]
    blk = blk.transpose(0, 1, 3, 2, 4).reshape(b, n_cmp, N_KV, CMP_LEN * HEAD_DIM)
    return jax.nn.gelu(blk @ w1) @ w2


def nsa_attention(q, kc_raw, vc_raw, ks, vs, kw, vw, gates, pe_k, w1k, w2k, pe_v, w1v, w2v):
    b, s = q.shape[0], q.shape[1]
    pos = jnp.arange(s)
    n_blk = s // Q_BLOCK
    n_selb = s // SEL_BLOCK
    n_sel = min(N_SEL, n_selb)
    q = rope(q, pos) * (HEAD_DIM ** -0.5)
    kc = compress(kc_raw, pe_k, w1k, w2k)
    vc = compress(vc_raw, pe_v, w1v, w2v)
    n_cmp = kc.shape[1]
    cmp_start = jnp.arange(n_cmp) * CMP_STRIDE
    cmp_end = cmp_start + CMP_LEN - 1
    kc = rope(kc, cmp_end).transpose(0, 2, 1, 3)
    vc = vc.transpose(0, 2, 1, 3)
    sel_ids = jnp.arange(n_selb)
    sel_start = sel_ids * SEL_BLOCK
    overlap = ((cmp_start[:, None] < sel_start[None, :] + SEL_BLOCK)
               & (cmp_start[:, None] + CMP_LEN > sel_start[None, :])).astype(jnp.float32)
    ks_blk = rope(ks, pos).reshape(b, n_selb, SEL_BLOCK, N_KV, HEAD_DIM).transpose(0, 3, 1, 2, 4)
    vs_blk = vs.reshape(b, n_selb, SEL_BLOCK, N_KV, HEAD_DIM).transpose(0, 3, 1, 2, 4)
    kw_pad = jnp.pad(rope(kw, pos), ((0, 0), (WINDOW, 0), (0, 0), (0, 0))).transpose(0, 2, 1, 3)
    vw_pad = jnp.pad(vw, ((0, 0), (WINDOW, 0), (0, 0), (0, 0))).transpose(0, 2, 1, 3)
    q_blocks = q.reshape(b, n_blk, Q_BLOCK, N_KV, HPG, HEAD_DIM).transpose(1, 0, 3, 4, 2, 5)
    g_blocks = jax.nn.sigmoid(gates.astype(jnp.float32)).reshape(
        b, n_blk, Q_BLOCK, N_KV, HPG, 3).transpose(1, 0, 3, 4, 2, 5)
    b_ix = jnp.arange(b)[:, None, None, None]
    g_ix = jnp.arange(N_KV)[None, :, None, None]

    def block(args):
        blk, qb, gb = args
        t = blk * Q_BLOCK + jnp.arange(Q_BLOCK)
        s_c = jnp.einsum('bghqd,bgnd->bghqn', qb, kc).astype(jnp.float32)
        p_c = masked_softmax(s_c, cmp_end[None, :] <= t[:, None])
        o_c = jnp.einsum('bghqn,bgnd->bghqd', p_c.astype(vc.dtype), vc)
        imp = jnp.einsum('bghqn,nm->bgqm', p_c, overlap)
        cur = (t // SEL_BLOCK)[:, None]
        valid = sel_start[None, :] <= t[:, None]
        forced = (sel_ids[None, :] == 0) | (sel_ids[None, :] == cur) | (sel_ids[None, :] == cur - 1)
        imp = jnp.where(forced, FORCE_SCORE, jnp.where(valid, imp, NEG_INF))
        _, sidx = lax.top_k(imp, n_sel)
        k_sel = ks_blk[b_ix, g_ix, sidx]
        v_sel = vs_blk[b_ix, g_ix, sidx]
        s_s = jnp.einsum('bghqd,bgqnkd->bghqnk', qb, k_sel).astype(jnp.float32)
        kpos = sidx[..., None] * SEL_BLOCK + jnp.arange(SEL_BLOCK)
        m_s = (kpos <= t[:, None, None])[:, :, None]
        shp = s_s.shape
        p_s = masked_softmax(s_s.reshape(shp[0], shp[1], shp[2], shp[3], -1),
                             m_s.reshape(shp[0], shp[1], 1, shp[3], -1)).reshape(shp)
        o_s = jnp.einsum('bghqnk,bgqnkd->bghqd', p_s.astype(v_sel.dtype), v_sel)
        start = blk * Q_BLOCK
        k_w = lax.dynamic_slice_in_dim(kw_pad, start, Q_BLOCK + WINDOW, axis=2)
        v_w = lax.dynamic_slice_in_dim(vw_pad, start, Q_BLOCK + WINDOW, axis=2)
        kpos_w = start - WINDOW + jnp.arange(Q_BLOCK + WINDOW)
        m_w = ((kpos_w[None, :] >= 0) & (kpos_w[None, :] <= t[:, None])
               & (t[:, None] - kpos_w[None, :] < WINDOW))
        s_w = jnp.einsum('bghqd,bgkd->bghqk', qb, k_w).astype(jnp.float32)
        p_w = masked_softmax(s_w, m_w)
        o_w = jnp.einsum('bghqk,bgkd->bghqd', p_w.astype(v_w.dtype), v_w)
        gb = gb.astype(qb.dtype)
        return gb[..., 0:1] * o_c + gb[..., 1:2] * o_s + gb[..., 2:3] * o_w

    out = lax.map(block, (jnp.arange(n_blk), q_blocks, g_blocks))
    return out.transpose(1, 0, 4, 2, 3, 5).reshape(b, s, ATTN_WIDTH)


KV_TILE = 256
BLOCKS_PER_TILE = KV_TILE // SEL_BLOCK
MASK_BIAS = 1e30
VMEM_LIMIT = 56 * 1024 * 1024


def _nsa_kernel(qT_ref, gate_ref, kc_ref, vcT_ref, ks_ref, vsT_ref, kw_ref, vwT_ref, o_ref,
                ps_ref, sel_ref, *, tq, n_sel):
    f32, bf16 = jnp.float32, jnp.bfloat16
    i = pl.program_id(1)
    qT = qT_ref[0, 0]
    ncol = qT.shape[1]
    ncp = kc_ref.shape[1]
    nsb = sel_ref.shape[0]
    t0 = i * tq
    t_col = t0 + (lax.broadcasted_iota(jnp.int32, (1, ncol), 1) & (tq - 1))
    t_tok = t0 + lax.broadcasted_iota(jnp.int32, (1, tq), 1)

    s = jnp.dot(kc_ref[0], qT, preferred_element_type=f32)
    n_idx = lax.broadcasted_iota(jnp.int32, (ncp, ncol), 0)
    cmask = (n_idx * CMP_STRIDE + (CMP_LEN - 1)) <= t_col
    s = jnp.where(cmask, s, NEG_INF)
    m = jnp.max(s, axis=0, keepdims=True)
    e = jnp.where(cmask, jnp.exp(s - m), 0.0)
    l = jnp.sum(e, axis=0, keepdims=True)
    p = e / jnp.maximum(l, 1e-30)
    o_c = jnp.dot(vcT_ref[0], p.astype(bf16), preferred_element_type=f32)

    psum = p[:, 0:tq]
    for h in range(1, HPG):
        psum = psum + p[:, h * tq:(h + 1) * tq]
    ps_ref[...] = psum
    parts = [ps_ref[pl.ds(k, nsb, stride=4), :] for k in range(4)]
    m_idx = lax.broadcasted_iota(jnp.int32, (nsb, tq), 0)
    prev = jnp.where(m_idx == 0, 0.0, pltpu.roll(parts[3], 1, axis=0))
    imp = parts[0] + parts[1] + parts[2] + parts[3] + prev

    cur = t_tok // SEL_BLOCK
    forced = (m_idx == 0) | (m_idx == cur) | (m_idx == cur - 1)
    valid = m_idx <= cur
    val0 = jnp.where(forced, FORCE_SCORE, jnp.where(valid, imp, NEG_INF))
    m_f = m_idx.astype(f32)

    def pick(_, carry):
        val, sel = carry
        mx = jnp.max(val, axis=0, keepdims=True)
        first = jnp.min(jnp.where(val == mx, m_f, 1e9), axis=0, keepdims=True)
        hit = m_f == first
        return jnp.where(hit, -3e38, val), jnp.where(hit, 1.0, sel)

    _, sel = lax.fori_loop(0, n_sel, pick, (val0, jnp.zeros((nsb, tq), f32)))
    sel_ref[...] = jnp.where(valid, sel, 0.0)

    def sweep(k_ref, vT_ref, j, carry, mask_fn):
        m_i, l_i, acc = carry
        s = jnp.dot(k_ref[0, j], qT, preferred_element_type=f32)
        s = mask_fn(s, j)
        m_new = jnp.maximum(m_i, jnp.max(s, axis=0, keepdims=True))
        alpha = jnp.exp(m_i - m_new)
        pj = jnp.exp(s - m_new)
        l_new = alpha * l_i + jnp.sum(pj, axis=0, keepdims=True)
        acc = alpha * acc + jnp.dot(vT_ref[0, j], pj.astype(bf16), preferred_element_type=f32)
        return m_new, l_new, acc

    def sel_bias(s, j):
        rows = []
        for b in range(BLOCKS_PER_TILE):
            r = sel_ref[pl.ds(j * BLOCKS_PER_TILE + b, 1), :]
            rows.append(jnp.broadcast_to((r - 1.0) * MASK_BIAS, (SEL_BLOCK, tq)))
        bias = jnp.concatenate(rows, axis=0)
        return s + jnp.concatenate([bias] * HPG, axis=1)

    def kpos_of(j):
        return j * KV_TILE + lax.broadcasted_iota(jnp.int32, (KV_TILE, ncol), 0)

    def sel_diag(s, j):
        return jnp.where(kpos_of(j) <= t_col, sel_bias(s, j), NEG_INF)

    def win_mask(s, j):
        kpos = kpos_of(j)
        return jnp.where((kpos <= t_col) & (t_col - kpos < WINDOW), s, NEG_INF)

    init = (jnp.full((1, ncol), NEG_INF, f32), jnp.zeros((1, ncol), f32), jnp.zeros((HEAD_DIM, ncol), f32))
    jd = t0 // KV_TILE
    carry = lax.fori_loop(0, jd, lambda j, c: sweep(ks_ref, vsT_ref, j, c, sel_bias), init)
    _, l_s, acc_s = sweep(ks_ref, vsT_ref, jd, carry, sel_diag)
    jw0 = jnp.maximum(t0 - (WINDOW - 1), 0) // KV_TILE
    _, l_w, acc_w = lax.fori_loop(jw0, jd + 1, lambda j, c: sweep(kw_ref, vwT_ref, j, c, win_mask), init)

    g = jax.nn.sigmoid(gate_ref[0, 0])
    o_ref[0, 0] = (g[0:1] * o_c + g[1:2] * (acc_s / jnp.maximum(l_s, 1e-30))
                   + g[2:3] * (acc_w / jnp.maximum(l_w, 1e-30)))


def _nsa_pallas(qT, gateT, kc, vcT, ks, vsT, kw, vwT, *, tq, n_sel):
    g_, nq, _, ncol = qT.shape
    ncp = kc.shape[1]
    nt = ks.shape[1]
    nsb = nt * BLOCKS_PER_TILE
    per_q = lambda g, i: (g, i, 0, 0)
    per_g3 = lambda g, i: (g, 0, 0)
    per_g4 = lambda g, i: (g, 0, 0, 0)
    return pl.pallas_call(
        functools.partial(_nsa_kernel, tq=tq, n_sel=n_sel),
        grid=(g_, nq),
        in_specs=[pl.BlockSpec((1, 1, HEAD_DIM, ncol), per_q),
                  pl.BlockSpec((1, 1, 3, ncol), per_q),
                  pl.BlockSpec((1, ncp, HEAD_DIM), per_g3),
                  pl.BlockSpec((1, HEAD_DIM, ncp), per_g3),
                  pl.BlockSpec((1, nt, KV_TILE, HEAD_DIM), per_g4),
                  pl.BlockSpec((1, nt, HEAD_DIM, KV_TILE), per_g4),
                  pl.BlockSpec((1, nt, KV_TILE, HEAD_DIM), per_g4),
                  pl.BlockSpec((1, nt, HEAD_DIM, KV_TILE), per_g4)],
        out_specs=pl.BlockSpec((1, 1, HEAD_DIM, ncol), per_q),
        out_shape=jax.ShapeDtypeStruct((g_, nq, HEAD_DIM, ncol), jnp.float32),
        scratch_shapes=[pltpu.VMEM((ncp, tq), jnp.float32), pltpu.VMEM((nsb, tq), jnp.float32)],
        compiler_params=pltpu.CompilerParams(dimension_semantics=("arbitrary", "arbitrary"),
                                             vmem_limit_bytes=VMEM_LIMIT),
        name="nsa_attention",
    )(qT, gateT, kc, vcT, ks, vsT, kw, vwT)


def nsa_attention_pallas(q, kc_raw, vc_raw, ks, vs, kw, vw, gates, pe_k, w1k, w2k, pe_v, w1v, w2v):
    b, s = q.shape[0], q.shape[1]
    assert b == 1 and s % KV_TILE == 0
    bf16 = jnp.bfloat16
    tq = Q_BLOCK
    nq, nt = s // tq, s // KV_TILE
    pos = jnp.arange(s)
    n_sel = min(N_SEL, s // SEL_BLOCK)
    q = rope(q, pos) * (HEAD_DIM ** -0.5)
    kc = compress(kc_raw, pe_k, w1k, w2k)
    vc = compress(vc_raw, pe_v, w1v, w2v)
    n_cmp = kc.shape[1]
    ncp = s // CMP_STRIDE
    kc = rope(kc, jnp.arange(n_cmp) * CMP_STRIDE + CMP_LEN - 1)
    pad = ((0, ncp - n_cmp), (0, 0), (0, 0))
    kc_g = jnp.pad(kc[0], pad).transpose(1, 0, 2).astype(bf16)
    vcT = jnp.pad(vc[0], pad).transpose(1, 2, 0).astype(bf16)
    tile_k = lambda k: k[0].transpose(1, 0, 2).reshape(N_KV, nt, KV_TILE, HEAD_DIM).astype(bf16)
    tile_vT = lambda v: v[0].transpose(1, 2, 0).reshape(N_KV, HEAD_DIM, nt, KV_TILE).transpose(0, 2, 1, 3).astype(bf16)
    qT = q[0].reshape(nq, tq, N_KV, HPG, HEAD_DIM).transpose(2, 0, 4, 3, 1).reshape(N_KV, nq, HEAD_DIM, HPG * tq)
    gateT = gates[0].reshape(nq, tq, N_KV, HPG, 3).transpose(2, 0, 4, 3, 1).reshape(N_KV, nq, 3, HPG * tq)
    oT = _nsa_pallas(qT.astype(bf16), gateT.astype(jnp.float32), kc_g, vcT,
                     tile_k(rope(ks, pos)), tile_vT(vs), tile_k(rope(kw, pos)), tile_vT(vw), tq=tq, n_sel=n_sel)
    out = oT.reshape(N_KV, nq, HEAD_DIM, HPG, tq).transpose(1, 4, 0, 3, 2)
    return out.reshape(b, s, ATTN_WIDTH)


def s5_ssm(u, lam_re, lam_im, log_dt, b_re, b_im, c_re, c_im, d_skip, w_glu):
    b, s = u.shape[0], u.shape[1]
    f32 = jnp.float32
    lam = lax.complex(lam_re.astype(f32), lam_im.astype(f32))
    dt = jnp.exp(log_dt.astype(f32))[:, None]
    lam_bar = jnp.exp(lam * dt)
    bmat = lax.complex(b_re.astype(f32), b_im.astype(f32))
    b_bar = ((lam_bar - 1.0) / lam)[..., None] * bmat
    cmat = lax.complex(c_re.astype(f32), c_im.astype(f32))
    uf = u.astype(f32)
    ug = uf.reshape(b, s, SSM_GROUPS, SSM_GROUP).astype(jnp.complex64)
    bu = jnp.einsum('gpc,bsgc->bsgp', b_bar, ug)
    a = jnp.broadcast_to(lam_bar, bu.shape)

    def combine(e1, e2):
        a1, b1 = e1
        a2, b2 = e2
        return (a1 * a2, a2 * b1 + b2)

    _, states = lax.associative_scan(combine, (a, bu), axis=1)
    y = jnp.real(jnp.einsum('gcp,bsgp->bsgc', cmat, states)).reshape(b, s, SSM_WIDTH)
    y = jax.nn.gelu(y + d_skip.astype(f32) * uf).astype(u.dtype)
    ab = y @ w_glu
    return ab[..., :SSM_WIDTH] * jax.nn.sigmoid(ab[..., SSM_WIDTH:])


def peer_ffn(x, w_q, sub1, sub2, u_tab, v_tab):
    b, s, d = x.shape
    n_ch = s // PEER_CHUNK
    half = PEER_QDIM // 2
    xc = x.reshape(b, n_ch, PEER_CHUNK, d).transpose(1, 0, 2, 3)

    def chunk(xb):
        q = (xb @ w_q).reshape(b, PEER_CHUNK, PEER_HEADS, PEER_QDIM)
        s1 = jnp.einsum('bchd,kd->bchk', q[..., :half], sub1).astype(jnp.float32)
        s2 = jnp.einsum('bchd,kd->bchk', q[..., half:], sub2).astype(jnp.float32)
        v1, i1 = lax.top_k(s1, PEER_TOPK)
        v2, i2 = lax.top_k(s2, PEER_TOPK)
        cand = (v1[..., :, None] + v2[..., None, :]).reshape(b, PEER_CHUNK, PEER_HEADS, PEER_TOPK * PEER_TOPK)
        cidx = (i1[..., :, None] * PEER_NKEYS + i2[..., None, :]).reshape(b, PEER_CHUNK, PEER_HEADS, PEER_TOPK * PEER_TOPK)
        top, sel = lax.top_k(cand, PEER_TOPK)
        eidx = jnp.take_along_axis(cidx, sel, axis=-1)
        gate = jax.nn.softmax(top, axis=-1)
        u_sel = u_tab[eidx]
        v_sel = v_tab[eidx]
        hid = jnp.einsum('bcd,bchkd->bchk', xb, u_sel).astype(jnp.float32)
        act = (jax.nn.gelu(hid) * gate).astype(xb.dtype)
        return jnp.einsum('bchk,bchkd->bcd', act, v_sel)

    out = lax.map(chunk, xc)
    return out.transpose(1, 0, 2, 3).reshape(b, s, d)


def _final_norm_kernel(h_ref, g_ref, o_ref):
    xf = h_ref[...]
    y = xf * lax.rsqrt(jnp.mean(xf * xf, axis=-1, keepdims=True) + NORM_EPS)
    o_ref[...] = y * g_ref[...]


def _final_norm_pallas(h, g):
    b, s, d = h.shape
    tb = 512
    out = pl.pallas_call(
        _final_norm_kernel,
        grid=(b * s // tb,),
        in_specs=[pl.BlockSpec((tb, d), lambda i: (i, 0)), pl.BlockSpec((1, d), lambda i: (0, 0))],
        out_specs=pl.BlockSpec((tb, d), lambda i: (i, 0)),
        out_shape=jax.ShapeDtypeStruct((b * s, d), h.dtype),
        name="final_norm",
    )(h.reshape(b * s, d), g.reshape(1, d))
    return out.reshape(b, s, d)


def kernel(x, attn_norm, w_in, cmp_k_pe, cmp_k_w1, cmp_k_w2, cmp_v_pe, cmp_v_w1, cmp_v_w2, ssm_lam_re, ssm_lam_im, ssm_log_dt, ssm_b_re, ssm_b_im, ssm_c_re, ssm_c_im, ssm_d, ssm_w_glu, w_out, ffn_norm, peer_w_q, peer_subkeys_1, peer_subkeys_2, peer_u, peer_v, final_norm):
    b, s, _ = x.shape
    sizes = [ATTN_WIDTH] + [KV_WIDTH] * 6 + [3 * N_HEADS]
    cuts = [int(c) for c in np.cumsum(sizes)]
    h = x
    for l in range(DEPTH):
        z = rms_norm(h, attn_norm[l])
        proj = z @ w_in[l]
        q, kc, vc, ksl, vsl, kwn, vwn, gates, u = jnp.split(proj, cuts, axis=-1)
        kv_shape = (b, s, N_KV, HEAD_DIM)
        a_out = nsa_attention_pallas(q.reshape(b, s, N_HEADS, HEAD_DIM),
                              kc.reshape(kv_shape), vc.reshape(kv_shape),
                              ksl.reshape(kv_shape), vsl.reshape(kv_shape),
                              kwn.reshape(kv_shape), vwn.reshape(kv_shape), gates,
                              cmp_k_pe[l], cmp_k_w1[l], cmp_k_w2[l],
                              cmp_v_pe[l], cmp_v_w1[l], cmp_v_w2[l])
        s_out = s5_ssm(u, ssm_lam_re[l], ssm_lam_im[l], ssm_log_dt[l], ssm_b_re[l], ssm_b_im[l],
                       ssm_c_re[l], ssm_c_im[l], ssm_d[l], ssm_w_glu[l])
        h = h + jnp.concatenate([a_out, s_out], axis=-1) @ w_out[l]
        h = h + peer_ffn(rms_norm(h, ffn_norm[l]), peer_w_q[l], peer_subkeys_1[l],
                         peer_subkeys_2[l], peer_u[l], peer_v[l])
    return _final_norm_pallas(h, final_norm)
```

```python
import functools
import math
import jax, jax.numpy as jnp
from jax import lax
import numpy as np
from jax.experimental import pallas as pl
from jax.experimental.pallas import tpu as pltpu

D_MODEL = 1024
DEPTH = 1
N_HEADS = 8
HEAD_DIM = 64
N_KV = 2
HPG = N_HEADS // N_KV
ATTN_WIDTH = N_HEADS * HEAD_DIM
SSM_GROUP = 16
SSM_GROUPS = 32
SSM_WIDTH = SSM_GROUP * SSM_GROUPS
SSM_STATE = 64
MIX_WIDTH = ATTN_WIDTH + SSM_WIDTH
KV_WIDTH = N_KV * HEAD_DIM
CMP_LEN = 32
CMP_STRIDE = 16
CMP_HIDDEN = 128
SEL_BLOCK = 64
N_SEL = 16
WINDOW = 512
Q_BLOCK = 128
ROPE_THETA = 500000.0
ROPE_DIM = HEAD_DIM // 4
PEER_HEADS = 8
PEER_NKEYS = 128
PEER_QDIM = 256
PEER_TOPK = 16
PEER_CHUNK = 128
NORM_EPS = 1e-6
NEG_INF = -1e30
FORCE_SCORE = 1e9


def rms_norm(x, g):
    xf = x.astype(jnp.float32)
    y = xf * lax.rsqrt(jnp.mean(xf * xf, axis=-1, keepdims=True) + NORM_EPS)
    return (y * g.astype(jnp.float32)).astype(x.dtype)


def rope(x, pos):
    half = ROPE_DIM // 2
    inv = ROPE_THETA ** (-jnp.arange(half, dtype=jnp.float32) * 2.0 / ROPE_DIM)
    ang = pos.astype(jnp.float32)[:, None] * inv[None, :]
    cos = jnp.cos(ang)[None, :, None, :]
    sin = jnp.sin(ang)[None, :, None, :]
    xr = x[..., :ROPE_DIM].astype(jnp.float32)
    x1, x2 = xr[..., :half], xr[..., half:]
    rot = jnp.concatenate([x1 * cos - x2 * sin, x2 * cos + x1 * sin], axis=-1).astype(x.dtype)
    return jnp.concatenate([rot, x[..., ROPE_DIM:]], axis=-1)


def masked_softmax(s, mask):
    s = jnp.where(mask, s, NEG_INF)
    m = jnp.max(s, axis=-1, keepdims=True)
    e = jnp.where(mask, jnp.exp(s - m), 0.0)
    return e / jnp.maximum(jnp.sum(e, axis=-1, keepdims=True), 1e-30)


def compress(kv, pe, w1, w2):
    b, s = kv.shape[0], kv.shape[1]
    n_cmp = (s - CMP_LEN) // CMP_STRIDE + 1
    idx = jnp.arange(n_cmp)[:, None] * CMP_STRIDE + jnp.arange(CMP_LEN)[None, :]
    blk = kv[:, idx] + pe[None, None, :, None, :]
    blk = blk.transpose(0, 1, 3, 2, 4).reshape(b, n_cmp, N_KV, CMP_LEN * HEAD_DIM)
    return jax.nn.gelu(blk @ w1) @ w2


def nsa_attention(q, kc_raw, vc_raw, ks, vs, kw, vw, gates, pe_k, w1k, w2k, pe_v, w1v, w2v):
    b, s = q.shape[0], q.shape[1]
    pos = jnp.arange(s)
    n_blk = s // Q_BLOCK
    n_selb = s // SEL_BLOCK
    n_sel = min(N_SEL, n_selb)
    q = rope(q, pos) * (HEAD_DIM ** -0.5)
    kc = compress(kc_raw, pe_k, w1k, w2k)
    vc = compress(vc_raw, pe_v, w1v, w2v)
    n_cmp = kc.shape[1]
    cmp_start = jnp.arange(n_cmp) * CMP_STRIDE
    cmp_end = cmp_start + CMP_LEN - 1
    kc = rope(kc, cmp_end).transpose(0, 2, 1, 3)
    vc = vc.transpose(0, 2, 1, 3)
    sel_ids = jnp.arange(n_selb)
    sel_start = sel_ids * SEL_BLOCK
    overlap = ((cmp_start[:, None] < sel_start[None, :] + SEL_BLOCK)
               & (cmp_start[:, None] + CMP_LEN > sel_start[None, :])).astype(jnp.float32)
    ks_blk = rope(ks, pos).reshape(b, n_selb, SEL_BLOCK, N_KV, HEAD_DIM).transpose(0, 3, 1, 2, 4)
    vs_blk = vs.reshape(b, n_selb, SEL_BLOCK, N_KV, HEAD_DIM).transpose(0, 3, 1, 2, 4)
    kw_pad = jnp.pad(rope(kw, pos), ((0, 0), (WINDOW, 0), (0, 0), (0, 0))).transpose(0, 2, 1, 3)
    vw_pad = jnp.pad(vw, ((0, 0), (WINDOW, 0), (0, 0), (0, 0))).transpose(0, 2, 1, 3)
    q_blocks = q.reshape(b, n_blk, Q_BLOCK, N_KV, HPG, HEAD_DIM).transpose(1, 0, 3, 4, 2, 5)
    g_blocks = jax.nn.sigmoid(gates.astype(jnp.float32)).reshape(
        b, n_blk, Q_BLOCK, N_KV, HPG, 3).transpose(1, 0, 3, 4, 2, 5)
    b_ix = jnp.arange(b)[:, None, None, None]
    g_ix = jnp.arange(N_KV)[None, :, None, None]

    def block(args):
        blk, qb, gb = args
        t = blk * Q_BLOCK + jnp.arange(Q_BLOCK)
        s_c = jnp.einsum('bghqd,bgnd->bghqn', qb, kc).astype(jnp.float32)
        p_c = masked_softmax(s_c, cmp_end[None, :] <= t[:, None])
        o_c = jnp.einsum('bghqn,bgnd->bghqd', p_c.astype(vc.dtype), vc)
        imp = jnp.einsum('bghqn,nm->bgqm', p_c, overlap)
        cur = (t // SEL_BLOCK)[:, None]
        valid = sel_start[None, :] <= t[:, None]
        forced = (sel_ids[None, :] == 0) | (sel_ids[None, :] == cur) | (sel_ids[None, :] == cur - 1)
        imp = jnp.where(forced, FORCE_SCORE, jnp.where(valid, imp, NEG_INF))
        _, sidx = lax.top_k(imp, n_sel)
        k_sel = ks_blk[b_ix, g_ix, sidx]
        v_sel = vs_blk[b_ix, g_ix, sidx]
        s_s = jnp.einsum('bghqd,bgqnkd->bghqnk', qb, k_sel).astype(jnp.float32)
        kpos = sidx[..., None] * SEL_BLOCK + jnp.arange(SEL_BLOCK)
        m_s = (kpos <= t[:, None, None])[:, :, None]
        shp = s_s.shape
        p_s = masked_softmax(s_s.reshape(shp[0], shp[1], shp[2], shp[3], -1),
                             m_s.reshape(shp[0], shp[1], 1, shp[3], -1)).reshape(shp)
        o_s = jnp.einsum('bghqnk,bgqnkd->bghqd', p_s.astype(v_sel.dtype), v_sel)
        start = blk * Q_BLOCK
        k_w = lax.dynamic_slice_in_dim(kw_pad, start, Q_BLOCK + WINDOW, axis=2)
        v_w = lax.dynamic_slice_in_dim(vw_pad, start, Q_BLOCK + WINDOW, axis=2)
        kpos_w = start - WINDOW + jnp.arange(Q_BLOCK + WINDOW)
        m_w = ((kpos_w[None, :] >= 0) & (kpos_w[None, :] <= t[:, None])
               & (t[:, None] - kpos_w[None, :] < WINDOW))
        s_w = jnp.einsum('bghqd,bgkd->bghqk', qb, k_w).astype(jnp.float32)
        p_w = masked_softmax(s_w, m_w)
        o_w = jnp.einsum('bghqk,bgkd->bghqd', p_w.astype(v_w.dtype), v_w)
        gb = gb.astype(qb.dtype)
        return gb[..., 0:1] * o_c + gb[..., 1:2] * o_s + gb[..., 2:3] * o_w

    out = lax.map(block, (jnp.arange(n_blk), q_blocks, g_blocks))
    return out.transpose(1, 0, 4, 2, 3, 5).reshape(b, s, ATTN_WIDTH)


KV_TILE = 256
BLOCKS_PER_TILE = KV_TILE // SEL_BLOCK
MASK_BIAS = 1e30
VMEM_LIMIT = 56 * 1024 * 1024


def _nsa_kernel(qT_ref, gate_ref, kc_ref, vcT_ref, ks_ref, vsT_ref, kw_ref, vwT_ref, o_ref,
                ps_ref, sel_ref, *, tq, n_sel):
    f32, bf16 = jnp.float32, jnp.bfloat16
    i = pl.program_id(1)
    qT = qT_ref[0, 0]
    ncol = qT.shape[1]
    ncp = kc_ref.shape[1]
    nsb = sel_ref.shape[0]
    t0 = i * tq
    t_col = t0 + (lax.broadcasted_iota(jnp.int32, (1, ncol), 1) & (tq - 1))
    t_tok = t0 + lax.broadcasted_iota(jnp.int32, (1, tq), 1)

    s = jnp.dot(kc_ref[0], qT, preferred_element_type=f32)
    n_idx = lax.broadcasted_iota(jnp.int32, (ncp, ncol), 0)
    cmask = (n_idx * CMP_STRIDE + (CMP_LEN - 1)) <= t_col
    s = jnp.where(cmask, s, NEG_INF)
    m = jnp.max(s, axis=0, keepdims=True)
    e = jnp.where(cmask, jnp.exp(s - m), 0.0)
    l = jnp.sum(e, axis=0, keepdims=True)
    p = e / jnp.maximum(l, 1e-30)
    o_c = jnp.dot(vcT_ref[0], p.astype(bf16), preferred_element_type=f32)

    psum = p[:, 0:tq]
    for h in range(1, HPG):
        psum = psum + p[:, h * tq:(h + 1) * tq]
    ps_ref[...] = psum
    parts = [ps_ref[pl.ds(k, nsb, stride=4), :] for k in range(4)]
    m_idx = lax.broadcasted_iota(jnp.int32, (nsb, tq), 0)
    prev = jnp.where(m_idx == 0, 0.0, pltpu.roll(parts[3], 1, axis=0))
    imp = parts[0] + parts[1] + parts[2] + parts[3] + prev

    cur = t_tok // SEL_BLOCK
    forced = (m_idx == 0) | (m_idx == cur) | (m_idx == cur - 1)
    valid = m_idx <= cur
    val0 = jnp.where(forced, FORCE_SCORE, jnp.where(valid, imp, NEG_INF))
    m_f = m_idx.astype(f32)

    def pick(_, carry):
        val, sel = carry
        mx = jnp.max(val, axis=0, keepdims=True)
        first = jnp.min(jnp.where(val == mx, m_f, 1e9), axis=0, keepdims=True)
        hit = m_f == first
        return jnp.where(hit, -3e38, val), jnp.where(hit, 1.0, sel)

    _, sel = lax.fori_loop(0, n_sel, pick, (val0, jnp.zeros((nsb, tq), f32)))
    sel_ref[...] = jnp.where(valid, sel, 0.0)

    def sweep(k_ref, vT_ref, j, carry, mask_fn):
        m_i, l_i, acc = carry
        s = jnp.dot(k_ref[0, j], qT, preferred_element_type=f32)
        s = mask_fn(s, j)
        m_new = jnp.maximum(m_i, jnp.max(s, axis=0, keepdims=True))
        alpha = jnp.exp(m_i - m_new)
        pj = jnp.exp(s - m_new)
        l_new = alpha * l_i + jnp.sum(pj, axis=0, keepdims=True)
        acc = alpha * acc + jnp.dot(vT_ref[0, j], pj.astype(bf16), preferred_element_type=f32)
        return m_new, l_new, acc

    def sel_bias(s, j):
        rows = []
        for b in range(BLOCKS_PER_TILE):
            r = sel_ref[pl.ds(j * BLOCKS_PER_TILE + b, 1), :]
            rows.append(jnp.broadcast_to((r - 1.0) * MASK_BIAS, (SEL_BLOCK, tq)))
        bias = jnp.concatenate(rows, axis=0)
        return s + jnp.concatenate([bias] * HPG, axis=1)

    def kpos_of(j):
        return j * KV_TILE + lax.broadcasted_iota(jnp.int32, (KV_TILE, ncol), 0)

    def sel_diag(s, j):
        return jnp.where(kpos_of(j) <= t_col, sel_bias(s, j), NEG_INF)

    def win_mask(s, j):
        kpos = kpos_of(j)
        return jnp.where((kpos <= t_col) & (t_col - kpos < WINDOW), s, NEG_INF)

    init = (jnp.full((1, ncol), NEG_INF, f32), jnp.zeros((1, ncol), f32), jnp.zeros((HEAD_DIM, ncol), f32))
    jd = t0 // KV_TILE
    carry = lax.fori_loop(0, jd, lambda j, c: sweep(ks_ref, vsT_ref, j, c, sel_bias), init)
    _, l_s, acc_s = sweep(ks_ref, vsT_ref, jd, carry, sel_diag)
    jw0 = jnp.maximum(t0 - (WINDOW - 1), 0) // KV_TILE
    _, l_w, acc_w = lax.fori_loop(jw0, jd + 1, lambda j, c: sweep(kw_ref, vwT_ref, j, c, win_mask), init)

    g = jax.nn.sigmoid(gate_ref[0, 0])
    o_ref[0, 0] = (g[0:1] * o_c + g[1:2] * (acc_s / jnp.maximum(l_s, 1e-30))
                   + g[2:3] * (acc_w / jnp.maximum(l_w, 1e-30)))


def _nsa_pallas(qT, gateT, kc, vcT, ks, vsT, kw, vwT, *, tq, n_sel):
    g_, nq, _, ncol = qT.shape
    ncp = kc.shape[1]
    nt = ks.shape[1]
    nsb = nt * BLOCKS_PER_TILE
    per_q = lambda g, i: (g, i, 0, 0)
    per_g3 = lambda g, i: (g, 0, 0)
    per_g4 = lambda g, i: (g, 0, 0, 0)
    return pl.pallas_call(
        functools.partial(_nsa_kernel, tq=tq, n_sel=n_sel),
        grid=(g_, nq),
        in_specs=[pl.BlockSpec((1, 1, HEAD_DIM, ncol), per_q),
                  pl.BlockSpec((1, 1, 3, ncol), per_q),
                  pl.BlockSpec((1, ncp, HEAD_DIM), per_g3),
                  pl.BlockSpec((1, HEAD_DIM, ncp), per_g3),
                  pl.BlockSpec((1, nt, KV_TILE, HEAD_DIM), per_g4),
                  pl.BlockSpec((1, nt, HEAD_DIM, KV_TILE), per_g4),
                  pl.BlockSpec((1, nt, KV_TILE, HEAD_DIM), per_g4),
                  pl.BlockSpec((1, nt, HEAD_DIM, KV_TILE), per_g4)],
        out_specs=pl.BlockSpec((1, 1, HEAD_DIM, ncol), per_q),
        out_shape=jax.ShapeDtypeStruct((g_, nq, HEAD_DIM, ncol), jnp.float32),
        scratch_shapes=[pltpu.VMEM((ncp, tq), jnp.float32), pltpu.VMEM((nsb, tq), jnp.float32)],
        compiler_params=pltpu.CompilerParams(dimension_semantics=("arbitrary", "arbitrary"),
                                             vmem_limit_bytes=VMEM_LIMIT),
        name="nsa_attention",
    )(qT, gateT, kc, vcT, ks, vsT, kw, vwT)


def nsa_attention_pallas(q, kc_raw, vc_raw, ks, vs, kw, vw, gates, pe_k, w1k, w2k, pe_v, w1v, w2v):
    b, s = q.shape[0], q.shape[1]
    assert b == 1 and s % KV_TILE == 0
    bf16 = jnp.bfloat16
    tq = Q_BLOCK
    nq, nt = s // tq, s // KV_TILE
    pos = jnp.arange(s)
    n_sel = min(N_SEL, s // SEL_BLOCK)
    q = rope(q, pos) * (HEAD_DIM ** -0.5)
    kc = compress(kc_raw, pe_k, w1k, w2k)
    vc = compress(vc_raw, pe_v, w1v, w2v)
    n_cmp = kc.shape[1]
    ncp = s // CMP_STRIDE
    kc = rope(kc, jnp.arange(n_cmp) * CMP_STRIDE + CMP_LEN - 1)
    pad = ((0, ncp - n_cmp), (0, 0), (0, 0))
    kc_g = jnp.pad(kc[0], pad).transpose(1, 0, 2).astype(bf16)
    vcT = jnp.pad(vc[0], pad).transpose(1, 2, 0).astype(bf16)
    tile_k = lambda k: k[0].transpose(1, 0, 2).reshape(N_KV, nt, KV_TILE, HEAD_DIM).astype(bf16)
    tile_vT = lambda v: v[0].transpose(1, 2, 0).reshape(N_KV, HEAD_DIM, nt, KV_TILE).transpose(0, 2, 1, 3).astype(bf16)
    qT = q[0].reshape(nq, tq, N_KV, HPG, HEAD_DIM).transpose(2, 0, 4, 3, 1).reshape(N_KV, nq, HEAD_DIM, HPG * tq)
    gateT = gates[0].reshape(nq, tq, N_KV, HPG, 3).transpose(2, 0, 4, 3, 1).reshape(N_KV, nq, 3, HPG * tq)
    oT = _nsa_pallas(qT.astype(bf16), gateT.astype(jnp.float32), kc_g, vcT,
                     tile_k(rope(ks, pos)), tile_vT(vs), tile_k(rope(kw, pos)), tile_vT(vw), tq=tq, n_sel=n_sel)
    out = oT.reshape(N_KV, nq, HEAD_DIM, HPG, tq).transpose(1, 4, 0, 3, 2)
    return out.reshape(b, s, ATTN_WIDTH)


def s5_ssm(u, lam_re, lam_im, log_dt, b_re, b_im, c_re, c_im, d_skip, w_glu):
    b, s = u.shape[0], u.shape[1]
    f32 = jnp.float32
    lam = lax.complex(lam_re.astype(f32), lam_im.astype(f32))
    dt = jnp.exp(log_dt.astype(f32))[:, None]
    lam_bar = jnp.exp(lam * dt)
    bmat = lax.complex(b_re.astype(f32), b_im.astype(f32))
    b_bar = ((lam_bar - 1.0) / lam)[..., None] * bmat
    cmat = lax.complex(c_re.astype(f32), c_im.astype(f32))
    uf = u.astype(f32)
    ug = uf.reshape(b, s, SSM_GROUPS, SSM_GROUP).astype(jnp.complex64)
    bu = jnp.einsum('gpc,bsgc->bsgp', b_bar, ug)
    a = jnp.broadcast_to(lam_bar, bu.shape)

    def combine(e1, e2):
        a1, b1 = e1
        a2, b2 = e2
        return (a1 * a2, a2 * b1 + b2)

    _, states = lax.associative_scan(combine, (a, bu), axis=1)
    y = jnp.real(jnp.einsum('gcp,bsgp->bsgc', cmat, states)).reshape(b, s, SSM_WIDTH)
    y = jax.nn.gelu(y + d_skip.astype(f32) * uf).astype(u.dtype)
    ab = y @ w_glu
    return ab[..., :SSM_WIDTH] * jax.nn.sigmoid(ab[..., SSM_WIDTH:])


def peer_ffn(x, w_q, sub1, sub2, u_tab, v_tab):
    b, s, d = x.shape
    n_ch = s // PEER_CHUNK
    half = PEER_QDIM // 2
    xc = x.reshape(b, n_ch, PEER_CHUNK, d).transpose(1, 0, 2, 3)

    def chunk(xb):
        q = (xb @ w_q).reshape(b, PEER_CHUNK, PEER_HEADS, PEER_QDIM)
        s1 = jnp.einsum('bchd,kd->bchk', q[..., :half], sub1).astype(jnp.float32)
        s2 = jnp.einsum('bchd,kd->bchk', q[..., half:], sub2).astype(jnp.float32)
        v1, i1 = lax.top_k(s1, PEER_TOPK)
        v2, i2 = lax.top_k(s2, PEER_TOPK)
        cand = (v1[..., :, None] + v2[..., None, :]).reshape(b, PEER_CHUNK, PEER_HEADS, PEER_TOPK * PEER_TOPK)
        cidx = (i1[..., :, None] * PEER_NKEYS + i2[..., None, :]).reshape(b, PEER_CHUNK, PEER_HEADS, PEER_TOPK * PEER_TOPK)
        top, sel = lax.top_k(cand, PEER_TOPK)
        eidx = jnp.take_along_axis(cidx, sel, axis=-1)
        gate = jax.nn.softmax(top, axis=-1)
        u_sel = u_tab[eidx]
        v_sel = v_tab[eidx]
        hid = jnp.einsum('bcd,bchkd->bchk', xb, u_sel).astype(jnp.float32)
        act = (jax.nn.gelu(hid) * gate).astype(xb.dtype)
        return jnp.einsum('bchk,bchkd->bcd', act, v_sel)

    out = lax.map(chunk, xc)
    return out.transpose(1, 0, 2, 3).reshape(b, s, d)


PEER_HALF = PEER_QDIM // 2
AUX_TAU, AUX_M1, AUX_M2, AUX_Z, AUX_ROWS = PEER_TOPK, 2 * PEER_TOPK, 2 * PEER_TOPK + 1, 2 * PEER_TOPK + 2, 40
BIG = 3e38


def _top_values(val, k, store_row):
    r_f = lax.broadcasted_iota(jnp.int32, val.shape, 0).astype(jnp.float32)

    def body(a, v):
        mx = jnp.max(v, axis=0, keepdims=True)
        first = jnp.min(jnp.where(v == mx, r_f, 1e9), axis=0, keepdims=True)
        store_row(a, mx)
        return jnp.where(r_f == first, -BIG, v)

    lax.fori_loop(0, k, body, val)


def _peer_score_kernel(h_ref, g_ref, wqT_ref, sub1_ref, sub2_ref, xnT_ref, sT_ref, aux_ref, q_scr, v_scr, c_scr):
    f32, bf16 = jnp.float32, jnp.bfloat16
    k = PEER_TOPK
    x = h_ref[...]
    xn = x * lax.rsqrt(jnp.mean(x * x, axis=-1, keepdims=True) + NORM_EPS) * g_ref[...]
    xnb = xn.astype(bf16)
    xnT_ref[...] = xn.T.astype(bf16)
    q_scr[...] = lax.dot_general(wqT_ref[...], xnb, (((1,), (1,)), ((), ())), preferred_element_type=f32)
    tt = x.shape[0]

    def head(h, _):
        base = pl.multiple_of(h * PEER_QDIM, PEER_QDIM)
        s1 = jnp.dot(sub1_ref[...], q_scr[pl.ds(base, PEER_HALF), :].astype(bf16), preferred_element_type=f32)
        s2 = jnp.dot(sub2_ref[...], q_scr[pl.ds(base + PEER_HALF, PEER_HALF), :].astype(bf16), preferred_element_type=f32)
        sT_ref[h, 0:PEER_NKEYS, :] = s1
        sT_ref[h, PEER_NKEYS:2 * PEER_NKEYS, :] = s2

        def put(off):
            def store(a, row):
                v_scr[pl.ds(off + a, 1), :] = row
            return store

        _top_values(s1, k, put(0))
        _top_values(s2, k, put(k))
        v1 = v_scr[0:k, :]
        v2 = v_scr[k:2 * k, :]
        cand = [v1[0:1] + v2] + [v1[a:a + 1] + v2[0:8] for a in range(1, 8)] + [v1[8:k] + v2[0:1]]
        cand = jnp.concatenate(cand, axis=0)

        def keep_last(a, row):
            c_scr[...] = row

        _top_values(cand, k, keep_last)
        thr = c_scr[...]
        m1, m2 = v1[0:1], v2[0:1]
        z = jnp.sum(jnp.where(cand >= thr, jnp.exp(cand - (m1 + m2)), 0.0), axis=0, keepdims=True)
        taus = [jnp.min(jnp.where(v1[a:a + 1] + v2 >= thr, v2, BIG), axis=0, keepdims=True) for a in range(k)]
        pad = jnp.zeros((AUX_ROWS - AUX_Z - 1, tt), f32)
        aux_ref[h] = jnp.concatenate([v1] + taus + [m1, m2, z, pad], axis=0)
        return 0

    lax.fori_loop(0, PEER_HEADS, head, 0)


def _peer_expert_kernel(xnT_ref, sT_ref, aux_ref, u_ref, vT_ref, o_ref, e1_scr, e2_scr, tau_scr):
    f32, bf16 = jnp.float32, jnp.bfloat16
    e = pl.program_id(1)
    tt = xnT_ref.shape[1]
    n_i = u_ref.shape[0] // PEER_NKEYS

    @pl.when(e == 0)
    def _():
        o_ref[...] = jnp.zeros_like(o_ref)

        def head(h, _):
            aux = aux_ref[h]
            s1 = sT_ref[h, 0:PEER_NKEYS, :]
            s2 = sT_ref[h, PEER_NKEYS:2 * PEER_NKEYS, :]
            e1_scr[h] = jnp.exp(s1 - aux[AUX_M1:AUX_M1 + 1])
            e2_scr[h] = jnp.exp(s2 - aux[AUX_M2:AUX_M2 + 1]) / aux[AUX_Z:AUX_Z + 1]
            tau = jnp.full((PEER_NKEYS, tt), BIG, f32)
            for a in range(PEER_TOPK):
                tau = jnp.where(s1 == aux[a:a + 1], aux[AUX_TAU + a:AUX_TAU + a + 1], tau)
            tau_scr[h] = tau
            return 0

        lax.fori_loop(0, PEER_HEADS, head, 0)

    hid = jnp.dot(u_ref[...], xnT_ref[...], preferred_element_type=f32)
    act = jax.nn.gelu(hid)
    parts = []
    for ii in range(n_i):
        i = e * n_i + ii

        w = jnp.zeros((PEER_NKEYS, tt), f32)
        for h in range(PEER_HEADS):
            tau_row = tau_scr[h, pl.ds(i, 1), :]
            e1_row = e1_scr[h, pl.ds(i, 1), :]
            w = w + jnp.where(sT_ref[h, PEER_NKEYS:2 * PEER_NKEYS, :] >= tau_row, e2_scr[h], 0.0) * e1_row
        parts.append((act[ii * PEER_NKEYS:(ii + 1) * PEER_NKEYS] * w).astype(bf16))
    a_blk = jnp.concatenate(parts, axis=0)
    o_ref[...] += jnp.dot(vT_ref[...], a_blk, preferred_element_type=f32)


PEER_TOK_TILE = 512
PEER_EXP_TILE = 256


def peer_ffn_pallas(h1, g, w_q, sub1, sub2, u_tab, v_tab):
    f32, bf16 = jnp.float32, jnp.bfloat16
    s, d = h1.shape
    n_exp = u_tab.shape[0]
    tt = min(PEER_TOK_TILE, s)
    nq = PEER_HEADS * PEER_QDIM
    xnT, sT, aux = pl.pallas_call(
        _peer_score_kernel,
        grid=(s // tt,),
        in_specs=[pl.BlockSpec((tt, d), lambda t: (t, 0)),
                  pl.BlockSpec((1, d), lambda t: (0, 0)),
                  pl.BlockSpec((nq, d), lambda t: (0, 0)),
                  pl.BlockSpec((PEER_NKEYS, PEER_HALF), lambda t: (0, 0)),
                  pl.BlockSpec((PEER_NKEYS, PEER_HALF), lambda t: (0, 0))],
        out_specs=[pl.BlockSpec((d, tt), lambda t: (0, t)),
                   pl.BlockSpec((PEER_HEADS, 2 * PEER_NKEYS, tt), lambda t: (0, 0, t)),
                   pl.BlockSpec((PEER_HEADS, AUX_ROWS, tt), lambda t: (0, 0, t))],
        out_shape=[jax.ShapeDtypeStruct((d, s), bf16),
                   jax.ShapeDtypeStruct((PEER_HEADS, 2 * PEER_NKEYS, s), f32),
                   jax.ShapeDtypeStruct((PEER_HEADS, AUX_ROWS, s), f32)],
        scratch_shapes=[pltpu.VMEM((nq, tt), f32), pltpu.VMEM((2 * PEER_TOPK, tt), f32), pltpu.VMEM((1, tt), f32)],
        compiler_params=pltpu.CompilerParams(dimension_semantics=("arbitrary",), vmem_limit_bytes=VMEM_LIMIT),
        name="peer_scores",
    )(h1, g.reshape(1, d), w_q.T.astype(bf16), sub1.astype(bf16), sub2.astype(bf16))
    te = PEER_EXP_TILE
    outT = pl.pallas_call(
        _peer_expert_kernel,
        grid=(s // tt, n_exp // te),
        in_specs=[pl.BlockSpec((d, tt), lambda t, e: (0, t)),
                  pl.BlockSpec((PEER_HEADS, 2 * PEER_NKEYS, tt), lambda t, e: (0, 0, t)),
                  pl.BlockSpec((PEER_HEADS, AUX_ROWS, tt), lambda t, e: (0, 0, t)),
                  pl.BlockSpec((te, d), lambda t, e: (e, 0)),
                  pl.BlockSpec((d, te), lambda t, e: (0, e))],
        out_specs=pl.BlockSpec((d, tt), lambda t, e: (0, t)),
        out_shape=jax.ShapeDtypeStruct((d, s), f32),
        scratch_shapes=[pltpu.VMEM((PEER_HEADS, PEER_NKEYS, tt), f32)] * 3,
        compiler_params=pltpu.CompilerParams(dimension_semantics=("arbitrary", "arbitrary"),
                                             vmem_limit_bytes=VMEM_LIMIT),
        name="peer_experts",
    )(xnT, sT, aux, u_tab.astype(bf16), v_tab.T.astype(bf16))
    return outT.T


def _final_norm_kernel(h_ref, g_ref, o_ref):
    xf = h_ref[...]
    y = xf * lax.rsqrt(jnp.mean(xf * xf, axis=-1, keepdims=True) + NORM_EPS)
    o_ref[...] = y * g_ref[...]


def _final_norm_pallas(h, g):
    b, s, d = h.shape
    tb = 512
    out = pl.pallas_call(
        _final_norm_kernel,
        grid=(b * s // tb,),
        in_specs=[pl.BlockSpec((tb, d), lambda i: (i, 0)), pl.BlockSpec((1, d), lambda i: (0, 0))],
        out_specs=pl.BlockSpec((tb, d), lambda i: (i, 0)),
        out_shape=jax.ShapeDtypeStruct((b * s, d), h.dtype),
        name="final_norm",
    )(h.reshape(b * s, d), g.reshape(1, d))
    return out.reshape(b, s, d)


def kernel(x, attn_norm, w_in, cmp_k_pe, cmp_k_w1, cmp_k_w2, cmp_v_pe, cmp_v_w1, cmp_v_w2, ssm_lam_re, ssm_lam_im, ssm_log_dt, ssm_b_re, ssm_b_im, ssm_c_re, ssm_c_im, ssm_d, ssm_w_glu, w_out, ffn_norm, peer_w_q, peer_subkeys_1, peer_subkeys_2, peer_u, peer_v, final_norm):
    b, s, _ = x.shape
    sizes = [ATTN_WIDTH] + [KV_WIDTH] * 6 + [3 * N_HEADS]
    cuts = [int(c) for c in np.cumsum(sizes)]
    h = x
    for l in range(DEPTH):
        z = rms_norm(h, attn_norm[l])
        proj = z @ w_in[l]
        q, kc, vc, ksl, vsl, kwn, vwn, gates, u = jnp.split(proj, cuts, axis=-1)
        kv_shape = (b, s, N_KV, HEAD_DIM)
        a_out = nsa_attention_pallas(q.reshape(b, s, N_HEADS, HEAD_DIM),
                              kc.reshape(kv_shape), vc.reshape(kv_shape),
                              ksl.reshape(kv_shape), vsl.reshape(kv_shape),
                              kwn.reshape(kv_shape), vwn.reshape(kv_shape), gates,
                              cmp_k_pe[l], cmp_k_w1[l], cmp_k_w2[l],
                              cmp_v_pe[l], cmp_v_w1[l], cmp_v_w2[l])
        s_out = s5_ssm(u, ssm_lam_re[l], ssm_lam_im[l], ssm_log_dt[l], ssm_b_re[l], ssm_b_im[l],
                       ssm_c_re[l], ssm_c_im[l], ssm_d[l], ssm_w_glu[l])
        h = h + jnp.concatenate([a_out, s_out], axis=-1) @ w_out[l]
        h = h + peer_ffn_pallas(h[0], ffn_norm[l], peer_w_q[l], peer_subkeys_1[l],
                                peer_subkeys_2[l], peer_u[l], peer_v[l])[None]
    return _final_norm_pallas(h, final_norm)
```

```python
import functools
import math
import jax, jax.numpy as jnp
from jax import lax
import numpy as np
from jax.experimental import pallas as pl
from jax.experimental.pallas import tpu as pltpu

D_MODEL = 1024
DEPTH = 1
N_HEADS = 8
HEAD_DIM = 64
N_KV = 2
HPG = N_HEADS // N_KV
ATTN_WIDTH = N_HEADS * HEAD_DIM
SSM_GROUP = 16
SSM_GROUPS = 32
SSM_WIDTH = SSM_GROUP * SSM_GROUPS
SSM_STATE = 64
MIX_WIDTH = ATTN_WIDTH + SSM_WIDTH
KV_WIDTH = N_KV * HEAD_DIM
CMP_LEN = 32
CMP_STRIDE = 16
CMP_HIDDEN = 128
SEL_BLOCK = 64
N_SEL = 16
WINDOW = 512
Q_BLOCK = 128
ROPE_THETA = 500000.0
ROPE_DIM = HEAD_DIM // 4
PEER_HEADS = 8
PEER_NKEYS = 128
PEER_QDIM = 256
PEER_TOPK = 16
PEER_CHUNK = 128
NORM_EPS = 1e-6
NEG_INF = -1e30
FORCE_SCORE = 1e9


def rms_norm(x, g):
    xf = x.astype(jnp.float32)
    y = xf * lax.rsqrt(jnp.mean(xf * xf, axis=-1, keepdims=True) + NORM_EPS)
    return (y * g.astype(jnp.float32)).astype(x.dtype)


def rope(x, pos):
    half = ROPE_DIM // 2
    inv = ROPE_THETA ** (-jnp.arange(half, dtype=jnp.float32) * 2.0 / ROPE_DIM)
    ang = pos.astype(jnp.float32)[:, None] * inv[None, :]
    cos = jnp.cos(ang)[None, :, None, :]
    sin = jnp.sin(ang)[None, :, None, :]
    xr = x[..., :ROPE_DIM].astype(jnp.float32)
    x1, x2 = xr[..., :half], xr[..., half:]
    rot = jnp.concatenate([x1 * cos - x2 * sin, x2 * cos + x1 * sin], axis=-1).astype(x.dtype)
    return jnp.concatenate([rot, x[..., ROPE_DIM:]], axis=-1)


def masked_softmax(s, mask):
    s = jnp.where(mask, s, NEG_INF)
    m = jnp.max(s, axis=-1, keepdims=True)
    e = jnp.where(mask, jnp.exp(s - m), 0.0)
    return e / jnp.maximum(jnp.sum(e, axis=-1, keepdims=True), 1e-30)


def compress(kv, pe, w1, w2):
    b, s = kv.shape[0], kv.shape[1]
    n_cmp = (s - CMP_LEN) // CMP_STRIDE + 1
    idx = jnp.arange(n_cmp)[:, None] * CMP_STRIDE + jnp.arange(CMP_LEN)[None, :]
    blk = kv[:, idx] + pe[None, None, :, None, :]
    blk = blk.transpose(0, 1, 3, 2, 4).reshape(b, n_cmp, N_KV, CMP_LEN * HEAD_DIM)
    return jax.nn.gelu(blk @ w1) @ w2


def nsa_attention(q, kc_raw, vc_raw, ks, vs, kw, vw, gates, pe_k, w1k, w2k, pe_v, w1v, w2v):
    b, s = q.shape[0], q.shape[1]
    pos = jnp.arange(s)
    n_blk = s // Q_BLOCK
    n_selb = s // SEL_BLOCK
    n_sel = min(N_SEL, n_selb)
    q = rope(q, pos) * (HEAD_DIM ** -0.5)
    kc = compress(kc_raw, pe_k, w1k, w2k)
    vc = compress(vc_raw, pe_v, w1v, w2v)
    n_cmp = kc.shape[1]
    cmp_start = jnp.arange(n_cmp) * CMP_STRIDE
    cmp_end = cmp_start + CMP_LEN - 1
    kc = rope(kc, cmp_end).transpose(0, 2, 1, 3)
    vc = vc.transpose(0, 2, 1, 3)
    sel_ids = jnp.arange(n_selb)
    sel_start = sel_ids * SEL_BLOCK
    overlap = ((cmp_start[:, None] < sel_start[None, :] + SEL_BLOCK)
               & (cmp_start[:, None] + CMP_LEN > sel_start[None, :])).astype(jnp.float32)
    ks_blk = rope(ks, pos).reshape(b, n_selb, SEL_BLOCK, N_KV, HEAD_DIM).transpose(0, 3, 1, 2, 4)
    vs_blk = vs.reshape(b, n_selb, SEL_BLOCK, N_KV, HEAD_DIM).transpose(0, 3, 1, 2, 4)
    kw_pad = jnp.pad(rope(kw, pos), ((0, 0), (WINDOW, 0), (0, 0), (0, 0))).transpose(0, 2, 1, 3)
    vw_pad = jnp.pad(vw, ((0, 0), (WINDOW, 0), (0, 0), (0, 0))).transpose(0, 2, 1, 3)
    q_blocks = q.reshape(b, n_blk, Q_BLOCK, N_KV, HPG, HEAD_DIM).transpose(1, 0, 3, 4, 2, 5)
    g_blocks = jax.nn.sigmoid(gates.astype(jnp.float32)).reshape(
        b, n_blk, Q_BLOCK, N_KV, HPG, 3).transpose(1, 0, 3, 4, 2, 5)
    b_ix = jnp.arange(b)[:, None, None, None]
    g_ix = jnp.arange(N_KV)[None, :, None, None]

    def block(args):
        blk, qb, gb = args
        t = blk * Q_BLOCK + jnp.arange(Q_BLOCK)
        s_c = jnp.einsum('bghqd,bgnd->bghqn', qb, kc).astype(jnp.float32)
        p_c = masked_softmax(s_c, cmp_end[None, :] <= t[:, None])
        o_c = jnp.einsum('bghqn,bgnd->bghqd', p_c.astype(vc.dtype), vc)
        imp = jnp.einsum('bghqn,nm->bgqm', p_c, overlap)
        cur = (t // SEL_BLOCK)[:, None]
        valid = sel_start[None, :] <= t[:, None]
        forced = (sel_ids[None, :] == 0) | (sel_ids[None, :] == cur) | (sel_ids[None, :] == cur - 1)
        imp = jnp.where(forced, FORCE_SCORE, jnp.where(valid, imp, NEG_INF))
        _, sidx = lax.top_k(imp, n_sel)
        k_sel = ks_blk[b_ix, g_ix, sidx]
        v_sel = vs_blk[b_ix, g_ix, sidx]
        s_s = jnp.einsum('bghqd,bgqnkd->bghqnk', qb, k_sel).astype(jnp.float32)
        kpos = sidx[..., None] * SEL_BLOCK + jnp.arange(SEL_BLOCK)
        m_s = (kpos <= t[:, None, None])[:, :, None]
        shp = s_s.shape
        p_s = masked_softmax(s_s.reshape(shp[0], shp[1], shp[2], shp[3], -1),
                             m_s.reshape(shp[0], shp[1], 1, shp[3], -1)).reshape(shp)
        o_s = jnp.einsum('bghqnk,bgqnkd->bghqd', p_s.astype(v_sel.dtype), v_sel)
        start = blk * Q_BLOCK
        k_w = lax.dynamic_slice_in_dim(kw_pad, start, Q_BLOCK + WINDOW, axis=2)
        v_w = lax.dynamic_slice_in_dim(vw_pad, start, Q_BLOCK + WINDOW, axis=2)
        kpos_w = start - WINDOW + jnp.arange(Q_BLOCK + WINDOW)
        m_w = ((kpos_w[None, :] >= 0) & (kpos_w[None, :] <= t[:, None])
               & (t[:, None] - kpos_w[None, :] < WINDOW))
        s_w = jnp.einsum('bghqd,bgkd->bghqk', qb, k_w).astype(jnp.float32)
        p_w = masked_softmax(s_w, m_w)
        o_w = jnp.einsum('bghqk,bgkd->bghqd', p_w.astype(v_w.dtype), v_w)
        gb = gb.astype(qb.dtype)
        return gb[..., 0:1] * o_c + gb[..., 1:2] * o_s + gb[..., 2:3] * o_w

    out = lax.map(block, (jnp.arange(n_blk), q_blocks, g_blocks))
    return out.transpose(1, 0, 4, 2, 3, 5).reshape(b, s, ATTN_WIDTH)


KV_TILE = 256
BLOCKS_PER_TILE = KV_TILE // SEL_BLOCK
MASK_BIAS = 1e30
VMEM_LIMIT = 56 * 1024 * 1024


def _nsa_kernel(qT_ref, gate_ref, kc_ref, vcT_ref, ks_ref, vsT_ref, kw_ref, vwT_ref, o_ref,
                ps_ref, sel_ref, *, tq, n_sel):
    f32, bf16 = jnp.float32, jnp.bfloat16
    i = pl.program_id(1)
    qT = qT_ref[0, 0]
    ncol = qT.shape[1]
    ncp = kc_ref.shape[1]
    nsb = sel_ref.shape[0]
    t0 = i * tq
    t_col = t0 + (lax.broadcasted_iota(jnp.int32, (1, ncol), 1) & (tq - 1))
    t_tok = t0 + lax.broadcasted_iota(jnp.int32, (1, tq), 1)

    s = jnp.dot(kc_ref[0], qT, preferred_element_type=f32)
    n_idx = lax.broadcasted_iota(jnp.int32, (ncp, ncol), 0)
    cmask = (n_idx * CMP_STRIDE + (CMP_LEN - 1)) <= t_col
    s = jnp.where(cmask, s, NEG_INF)
    m = jnp.max(s, axis=0, keepdims=True)
    e = jnp.where(cmask, jnp.exp(s - m), 0.0)
    l = jnp.sum(e, axis=0, keepdims=True)
    p = e / jnp.maximum(l, 1e-30)
    o_c = jnp.dot(vcT_ref[0], p.astype(bf16), preferred_element_type=f32)

    psum = p[:, 0:tq]
    for h in range(1, HPG):
        psum = psum + p[:, h * tq:(h + 1) * tq]
    ps_ref[...] = psum
    parts = [ps_ref[pl.ds(k, nsb, stride=4), :] for k in range(4)]
    m_idx = lax.broadcasted_iota(jnp.int32, (nsb, tq), 0)
    prev = jnp.where(m_idx == 0, 0.0, pltpu.roll(parts[3], 1, axis=0))
    imp = parts[0] + parts[1] + parts[2] + parts[3] + prev

    cur = t_tok // SEL_BLOCK
    forced = (m_idx == 0) | (m_idx == cur) | (m_idx == cur - 1)
    valid = m_idx <= cur
    val0 = jnp.where(forced, FORCE_SCORE, jnp.where(valid, imp, NEG_INF))
    m_f = m_idx.astype(f32)

    def pick(_, carry):
        val, sel = carry
        mx = jnp.max(val, axis=0, keepdims=True)
        first = jnp.min(jnp.where(val == mx, m_f, 1e9), axis=0, keepdims=True)
        hit = m_f == first
        return jnp.where(hit, -3e38, val), jnp.where(hit, 1.0, sel)

    _, sel = lax.fori_loop(0, n_sel, pick, (val0, jnp.zeros((nsb, tq), f32)))
    sel_ref[...] = jnp.where(valid, sel, 0.0)

    def sweep(k_ref, vT_ref, j, carry, mask_fn):
        m_i, l_i, acc = carry
        s = jnp.dot(k_ref[0, j], qT, preferred_element_type=f32)
        s = mask_fn(s, j)
        m_new = jnp.maximum(m_i, jnp.max(s, axis=0, keepdims=True))
        alpha = jnp.exp(m_i - m_new)
        pj = jnp.exp(s - m_new)
        l_new = alpha * l_i + jnp.sum(pj, axis=0, keepdims=True)
        acc = alpha * acc + jnp.dot(vT_ref[0, j], pj.astype(bf16), preferred_element_type=f32)
        return m_new, l_new, acc

    def sel_bias(s, j):
        rows = []
        for b in range(BLOCKS_PER_TILE):
            r = sel_ref[pl.ds(j * BLOCKS_PER_TILE + b, 1), :]
            rows.append(jnp.broadcast_to((r - 1.0) * MASK_BIAS, (SEL_BLOCK, tq)))
        bias = jnp.concatenate(rows, axis=0)
        return s + jnp.concatenate([bias] * HPG, axis=1)

    def kpos_of(j):
        return j * KV_TILE + lax.broadcasted_iota(jnp.int32, (KV_TILE, ncol), 0)

    def sel_diag(s, j):
        return jnp.where(kpos_of(j) <= t_col, sel_bias(s, j), NEG_INF)

    def win_mask(s, j):
        kpos = kpos_of(j)
        return jnp.where((kpos <= t_col) & (t_col - kpos < WINDOW), s, NEG_INF)

    init = (jnp.full((1, ncol), NEG_INF, f32), jnp.zeros((1, ncol), f32), jnp.zeros((HEAD_DIM, ncol), f32))
    jd = t0 // KV_TILE
    carry = lax.fori_loop(0, jd, lambda j, c: sweep(ks_ref, vsT_ref, j, c, sel_bias), init)
    _, l_s, acc_s = sweep(ks_ref, vsT_ref, jd, carry, sel_diag)
    jw0 = jnp.maximum(t0 - (WINDOW - 1), 0) // KV_TILE
    _, l_w, acc_w = lax.fori_loop(jw0, jd + 1, lambda j, c: sweep(kw_ref, vwT_ref, j, c, win_mask), init)

    g = jax.nn.sigmoid(gate_ref[0, 0])
    o_ref[0, 0] = (g[0:1] * o_c + g[1:2] * (acc_s / jnp.maximum(l_s, 1e-30))
                   + g[2:3] * (acc_w / jnp.maximum(l_w, 1e-30)))


def _nsa_pallas(qT, gateT, kc, vcT, ks, vsT, kw, vwT, *, tq, n_sel):
    g_, nq, _, ncol = qT.shape
    ncp = kc.shape[1]
    nt = ks.shape[1]
    nsb = nt * BLOCKS_PER_TILE
    per_q = lambda g, i: (g, i, 0, 0)
    per_g3 = lambda g, i: (g, 0, 0)
    per_g4 = lambda g, i: (g, 0, 0, 0)
    return pl.pallas_call(
        functools.partial(_nsa_kernel, tq=tq, n_sel=n_sel),
        grid=(g_, nq),
        in_specs=[pl.BlockSpec((1, 1, HEAD_DIM, ncol), per_q),
                  pl.BlockSpec((1, 1, 3, ncol), per_q),
                  pl.BlockSpec((1, ncp, HEAD_DIM), per_g3),
                  pl.BlockSpec((1, HEAD_DIM, ncp), per_g3),
                  pl.BlockSpec((1, nt, KV_TILE, HEAD_DIM), per_g4),
                  pl.BlockSpec((1, nt, HEAD_DIM, KV_TILE), per_g4),
                  pl.BlockSpec((1, nt, KV_TILE, HEAD_DIM), per_g4),
                  pl.BlockSpec((1, nt, HEAD_DIM, KV_TILE), per_g4)],
        out_specs=pl.BlockSpec((1, 1, HEAD_DIM, ncol), per_q),
        out_shape=jax.ShapeDtypeStruct((g_, nq, HEAD_DIM, ncol), jnp.float32),
        scratch_shapes=[pltpu.VMEM((ncp, tq), jnp.float32), pltpu.VMEM((nsb, tq), jnp.float32)],
        compiler_params=pltpu.CompilerParams(dimension_semantics=("arbitrary", "arbitrary"),
                                             vmem_limit_bytes=VMEM_LIMIT),
        name="nsa_attention",
    )(qT, gateT, kc, vcT, ks, vsT, kw, vwT)


def nsa_attention_pallas(q, kc_raw, vc_raw, ks, vs, kw, vw, gates, pe_k, w1k, w2k, pe_v, w1v, w2v):
    b, s = q.shape[0], q.shape[1]
    assert b == 1 and s % KV_TILE == 0
    bf16 = jnp.bfloat16
    tq = Q_BLOCK
    nq, nt = s // tq, s // KV_TILE
    pos = jnp.arange(s)
    n_sel = min(N_SEL, s // SEL_BLOCK)
    q = rope(q, pos) * (HEAD_DIM ** -0.5)
    kc = compress(kc_raw, pe_k, w1k, w2k)
    vc = compress(vc_raw, pe_v, w1v, w2v)
    n_cmp = kc.shape[1]
    ncp = s // CMP_STRIDE
    kc = rope(kc, jnp.arange(n_cmp) * CMP_STRIDE + CMP_LEN - 1)
    pad = ((0, ncp - n_cmp), (0, 0), (0, 0))
    kc_g = jnp.pad(kc[0], pad).transpose(1, 0, 2).astype(bf16)
    vcT = jnp.pad(vc[0], pad).transpose(1, 2, 0).astype(bf16)
    tile_k = lambda k: k[0].transpose(1, 0, 2).reshape(N_KV, nt, KV_TILE, HEAD_DIM).astype(bf16)
    tile_vT = lambda v: v[0].transpose(1, 2, 0).reshape(N_KV, HEAD_DIM, nt, KV_TILE).transpose(0, 2, 1, 3).astype(bf16)
    qT = q[0].reshape(nq, tq, N_KV, HPG, HEAD_DIM).transpose(2, 0, 4, 3, 1).reshape(N_KV, nq, HEAD_DIM, HPG * tq)
    gateT = gates[0].reshape(nq, tq, N_KV, HPG, 3).transpose(2, 0, 4, 3, 1).reshape(N_KV, nq, 3, HPG * tq)
    oT = _nsa_pallas(qT.astype(bf16), gateT.astype(jnp.float32), kc_g, vcT,
                     tile_k(rope(ks, pos)), tile_vT(vs), tile_k(rope(kw, pos)), tile_vT(vw), tq=tq, n_sel=n_sel)
    out = oT.reshape(N_KV, nq, HEAD_DIM, HPG, tq).transpose(1, 4, 0, 3, 2)
    return out.reshape(b, s, ATTN_WIDTH)


def s5_ssm(u, lam_re, lam_im, log_dt, b_re, b_im, c_re, c_im, d_skip, w_glu):
    b, s = u.shape[0], u.shape[1]
    f32 = jnp.float32
    lam = lax.complex(lam_re.astype(f32), lam_im.astype(f32))
    dt = jnp.exp(log_dt.astype(f32))[:, None]
    lam_bar = jnp.exp(lam * dt)
    bmat = lax.complex(b_re.astype(f32), b_im.astype(f32))
    b_bar = ((lam_bar - 1.0) / lam)[..., None] * bmat
    cmat = lax.complex(c_re.astype(f32), c_im.astype(f32))
    uf = u.astype(f32)
    ug = uf.reshape(b, s, SSM_GROUPS, SSM_GROUP).astype(jnp.complex64)
    bu = jnp.einsum('gpc,bsgc->bsgp', b_bar, ug)
    a = jnp.broadcast_to(lam_bar, bu.shape)

    def combine(e1, e2):
        a1, b1 = e1
        a2, b2 = e2
        return (a1 * a2, a2 * b1 + b2)

    _, states = lax.associative_scan(combine, (a, bu), axis=1)
    y = jnp.real(jnp.einsum('gcp,bsgp->bsgc', cmat, states)).reshape(b, s, SSM_WIDTH)
    y = jax.nn.gelu(y + d_skip.astype(f32) * uf).astype(u.dtype)
    ab = y @ w_glu
    return ab[..., :SSM_WIDTH] * jax.nn.sigmoid(ab[..., SSM_WIDTH:])


def peer_ffn(x, w_q, sub1, sub2, u_tab, v_tab):
    b, s, d = x.shape
    n_ch = s // PEER_CHUNK
    half = PEER_QDIM // 2
    xc = x.reshape(b, n_ch, PEER_CHUNK, d).transpose(1, 0, 2, 3)

    def chunk(xb):
        q = (xb @ w_q).reshape(b, PEER_CHUNK, PEER_HEADS, PEER_QDIM)
        s1 = jnp.einsum('bchd,kd->bchk', q[..., :half], sub1).astype(jnp.float32)
        s2 = jnp.einsum('bchd,kd->bchk', q[..., half:], sub2).astype(jnp.float32)
        v1, i1 = lax.top_k(s1, PEER_TOPK)
        v2, i2 = lax.top_k(s2, PEER_TOPK)
        cand = (v1[..., :, None] + v2[..., None, :]).reshape(b, PEER_CHUNK, PEER_HEADS, PEER_TOPK * PEER_TOPK)
        cidx = (i1[..., :, None] * PEER_NKEYS + i2[..., None, :]).reshape(b, PEER_CHUNK, PEER_HEADS, PEER_TOPK * PEER_TOPK)
        top, sel = lax.top_k(cand, PEER_TOPK)
        eidx = jnp.take_along_axis(cidx, sel, axis=-1)
        gate = jax.nn.softmax(top, axis=-1)
        u_sel = u_tab[eidx]
        v_sel = v_tab[eidx]
        hid = jnp.einsum('bcd,bchkd->bchk', xb, u_sel).astype(jnp.float32)
        act = (jax.nn.gelu(hid) * gate).astype(xb.dtype)
        return jnp.einsum('bchk,bchkd->bcd', act, v_sel)

    out = lax.map(chunk, xc)
    return out.transpose(1, 0, 2, 3).reshape(b, s, d)


SSM_CHUNK = 16
SSM_ROW = SSM_CHUNK * SSM_GROUP
HI = lax.Precision.HIGHEST


def _cplx_pow(lr_dt, li_dt, n):
    mag = jnp.exp(n * lr_dt)
    return mag * jnp.cos(n * li_dt), mag * jnp.sin(n * li_dt)


def _ssm_kernel(u_ref, ldt_ref, lr_row_ref, li_row_ref, lr_col_ref, li_col_ref,
                bre_t_ref, bim_t_ref, cre_t_ref, cim_t_ref, y_ref):
    f32, bf16 = jnp.float32, jnp.bfloat16
    L, C, P = SSM_CHUNK, SSM_GROUP, SSM_STATE
    nc = u_ref.shape[1]
    dt = jnp.exp(ldt_ref[0])

    def zoh(lr, li):
        ar, ai = _cplx_pow(lr * dt, li * dt, 1.0)
        den = lr * lr + li * li
        return ar, ai, ((ar - 1.0) * lr + ai * li) / den, (ai * lr - (ar - 1.0) * li) / den

    lr_r, li_r = lr_row_ref[0], li_row_ref[0]
    _, _, fr_r, fi_r = zoh(lr_r, li_r)
    bbt_re = fr_r * bre_t_ref[0] - fi_r * bim_t_ref[0]
    bbt_im = fr_r * bim_t_ref[0] + fi_r * bre_t_ref[0]
    lr_c, li_c = lr_col_ref[0], li_col_ref[0]
    lane = lax.broadcasted_iota(jnp.int32, (1, L * C), 1)
    step = (lane // C).astype(f32)
    tile = (lax.broadcasted_iota(jnp.int32, (C, L * C), 0) == (lane % C)).astype(f32)
    cmt_re = jnp.dot(cre_t_ref[0], tile, precision=HI, preferred_element_type=f32)
    cmt_im = jnp.dot(cim_t_ref[0], tile, precision=HI, preferred_element_type=f32)

    def c_times_pow(n):
        pr, pi = _cplx_pow(lr_c * dt, li_c * dt, n)
        return pr * cmt_re - pi * cmt_im, pr * cmt_im + pi * cmt_re

    w_re, w_im = c_times_pow(step)
    taps = (jnp.dot(bbt_re, w_re, precision=HI, preferred_element_type=f32)
            - jnp.dot(bbt_im, w_im, precision=HI, preferred_element_type=f32))
    slabs = []
    for r in range(L):
        sl = taps if r == 0 else jnp.where(lane >= r * C, pltpu.roll(taps, r * C, axis=1), 0.0)
        slabs.append(sl)
    t_mat = jnp.concatenate(slabs, axis=0).astype(bf16)

    row_step = (lax.broadcasted_iota(jnp.int32, (L * C, 1), 0) // C).astype(f32)
    pr, pi = _cplx_pow(lr_r * dt, li_r * dt, (L - 1.0) - row_step)
    bt_re = jnp.concatenate([bbt_re] * L, axis=0)
    bt_im = jnp.concatenate([bbt_im] * L, axis=0)
    m_re = (pr * bt_re - pi * bt_im).astype(bf16)
    m_im = (pr * bt_im + pi * bt_re).astype(bf16)
    n_re, n_im = c_times_pow(step + 1.0)

    u = u_ref[0]
    e_re = jnp.dot(u, m_re, preferred_element_type=f32)
    e_im = jnp.dot(u, m_im, preferred_element_type=f32)
    k_idx = lax.broadcasted_iota(jnp.int32, (nc, P), 0)
    qr, qi = _cplx_pow(lr_r * dt, li_r * dt, float(L))
    d = 1
    while d < nc:
        sr = jnp.where(k_idx >= d, pltpu.roll(e_re, d, axis=0), 0.0)
        si = jnp.where(k_idx >= d, pltpu.roll(e_im, d, axis=0), 0.0)
        e_re, e_im = e_re + qr * sr - qi * si, e_im + qr * si + qi * sr
        qr, qi = qr * qr - qi * qi, 2.0 * qr * qi
        d *= 2
    x_re = jnp.where(k_idx >= 1, pltpu.roll(e_re, 1, axis=0), 0.0)
    x_im = jnp.where(k_idx >= 1, pltpu.roll(e_im, 1, axis=0), 0.0)
    y = jnp.dot(u, t_mat, preferred_element_type=f32)
    y = y + jnp.dot(x_re.astype(bf16), n_re.astype(bf16), preferred_element_type=f32)
    y = y - jnp.dot(x_im.astype(bf16), n_im.astype(bf16), preferred_element_type=f32)
    y_ref[0] = y


def ssm_scan_pallas(u, lam_re, lam_im, log_dt, b_re, b_im, c_re, c_im):
    f32, bf16 = jnp.float32, jnp.bfloat16
    s = u.shape[0]
    G, C, P, L = SSM_GROUPS, SSM_GROUP, SSM_STATE, SSM_CHUNK
    nc = s // L
    ug = u.reshape(nc, L, G, C).transpose(2, 0, 1, 3).reshape(G, nc, L * C).astype(bf16)
    per_g = lambda g: (g, 0, 0)
    spec = lambda a, b: pl.BlockSpec((1, a, b), per_g)
    y = pl.pallas_call(
        _ssm_kernel,
        grid=(G,),
        in_specs=[spec(nc, L * C), spec(1, 1), spec(1, P), spec(1, P), spec(P, 1), spec(P, 1),
                  spec(C, P), spec(C, P), spec(P, C), spec(P, C)],
        out_specs=spec(nc, L * C),
        out_shape=jax.ShapeDtypeStruct((G, nc, L * C), f32),
        compiler_params=pltpu.CompilerParams(dimension_semantics=("arbitrary",), vmem_limit_bytes=VMEM_LIMIT),
        name="ssm_scan",
    )(ug, log_dt.reshape(G, 1, 1), lam_re.reshape(G, 1, P), lam_im.reshape(G, 1, P),
      lam_re.reshape(G, P, 1), lam_im.reshape(G, P, 1),
      b_re.transpose(0, 2, 1), b_im.transpose(0, 2, 1), c_re.transpose(0, 2, 1), c_im.transpose(0, 2, 1))
    return y.reshape(G, nc, L, C).transpose(1, 2, 0, 3).reshape(s, G * C)


def s5_ssm_pallas(u, lam_re, lam_im, log_dt, b_re, b_im, c_re, c_im, d_skip, w_glu):
    y = ssm_scan_pallas(u[0], lam_re, lam_im, log_dt, b_re, b_im, c_re, c_im)[None]
    y = jax.nn.gelu(y + d_skip.astype(jnp.float32) * u)
    ab = y @ w_glu
    return ab[..., :SSM_WIDTH] * jax.nn.sigmoid(ab[..., SSM_WIDTH:])


PEER_HALF = PEER_QDIM // 2
AUX_TAU, AUX_M1, AUX_M2, AUX_Z, AUX_ROWS = PEER_TOPK, 2 * PEER_TOPK, 2 * PEER_TOPK + 1, 2 * PEER_TOPK + 2, 40
BIG = 3e38


def _top_values(val, k, store_row):
    r_f = lax.broadcasted_iota(jnp.int32, val.shape, 0).astype(jnp.float32)

    def body(a, v):
        mx = jnp.max(v, axis=0, keepdims=True)
        first = jnp.min(jnp.where(v == mx, r_f, 1e9), axis=0, keepdims=True)
        store_row(a, mx)
        return jnp.where(r_f == first, -BIG, v)

    lax.fori_loop(0, k, body, val)


def _peer_score_kernel(h_ref, g_ref, wqT_ref, sub1_ref, sub2_ref, xnT_ref, sT_ref, aux_ref, q_scr, v_scr, c_scr):
    f32, bf16 = jnp.float32, jnp.bfloat16
    k = PEER_TOPK
    x = h_ref[...]
    xn = x * lax.rsqrt(jnp.mean(x * x, axis=-1, keepdims=True) + NORM_EPS) * g_ref[...]
    xnb = xn.astype(bf16)
    xnT_ref[...] = xn.T.astype(bf16)
    q_scr[...] = lax.dot_general(wqT_ref[...], xnb, (((1,), (1,)), ((), ())), preferred_element_type=f32)
    tt = x.shape[0]

    def head(h, _):
        base = pl.multiple_of(h * PEER_QDIM, PEER_QDIM)
        s1 = jnp.dot(sub1_ref[...], q_scr[pl.ds(base, PEER_HALF), :].astype(bf16), preferred_element_type=f32)
        s2 = jnp.dot(sub2_ref[...], q_scr[pl.ds(base + PEER_HALF, PEER_HALF), :].astype(bf16), preferred_element_type=f32)
        sT_ref[h, 0:PEER_NKEYS, :] = s1
        sT_ref[h, PEER_NKEYS:2 * PEER_NKEYS, :] = s2

        def put(off):
            def store(a, row):
                v_scr[pl.ds(off + a, 1), :] = row
            return store

        _top_values(s1, k, put(0))
        _top_values(s2, k, put(k))
        v1 = v_scr[0:k, :]
        v2 = v_scr[k:2 * k, :]
        cand = [v1[0:1] + v2] + [v1[a:a + 1] + v2[0:8] for a in range(1, 8)] + [v1[8:k] + v2[0:1]]
        cand = jnp.concatenate(cand, axis=0)

        def keep_last(a, row):
            c_scr[...] = row

        _top_values(cand, k, keep_last)
        thr = c_scr[...]
        m1, m2 = v1[0:1], v2[0:1]
        z = jnp.sum(jnp.where(cand >= thr, jnp.exp(cand - (m1 + m2)), 0.0), axis=0, keepdims=True)
        taus = [jnp.min(jnp.where(v1[a:a + 1] + v2 >= thr, v2, BIG), axis=0, keepdims=True) for a in range(k)]
        pad = jnp.zeros((AUX_ROWS - AUX_Z - 1, tt), f32)
        aux_ref[h] = jnp.concatenate([v1] + taus + [m1, m2, z, pad], axis=0)
        return 0

    lax.fori_loop(0, PEER_HEADS, head, 0)


def _peer_expert_kernel(xnT_ref, sT_ref, aux_ref, u_ref, vT_ref, o_ref, e1_scr, e2_scr, tau_scr):
    f32, bf16 = jnp.float32, jnp.bfloat16
    e = pl.program_id(1)
    tt = xnT_ref.shape[1]
    n_i = u_ref.shape[0] // PEER_NKEYS

    @pl.when(e == 0)
    def _():
        o_ref[...] = jnp.zeros_like(o_ref)

        def head(h, _):
            aux = aux_ref[h]
            s1 = sT_ref[h, 0:PEER_NKEYS, :]
            s2 = sT_ref[h, PEER_NKEYS:2 * PEER_NKEYS, :]
            e1_scr[h] = jnp.exp(s1 - aux[AUX_M1:AUX_M1 + 1])
            e2_scr[h] = jnp.exp(s2 - aux[AUX_M2:AUX_M2 + 1]) / aux[AUX_Z:AUX_Z + 1]
            tau = jnp.full((PEER_NKEYS, tt), BIG, f32)
            for a in range(PEER_TOPK):
                tau = jnp.where(s1 == aux[a:a + 1], aux[AUX_TAU + a:AUX_TAU + a + 1], tau)
            tau_scr[h] = tau
            return 0

        lax.fori_loop(0, PEER_HEADS, head, 0)

    hid = jnp.dot(u_ref[...], xnT_ref[...], preferred_element_type=f32)
    act = jax.nn.gelu(hid)
    parts = []
    for ii in range(n_i):
        i = e * n_i + ii

        w = jnp.zeros((PEER_NKEYS, tt), f32)
        for h in range(PEER_HEADS):
            tau_row = tau_scr[h, pl.ds(i, 1), :]
            e1_row = e1_scr[h, pl.ds(i, 1), :]
            w = w + jnp.where(sT_ref[h, PEER_NKEYS:2 * PEER_NKEYS, :] >= tau_row, e2_scr[h], 0.0) * e1_row
        parts.append((act[ii * PEER_NKEYS:(ii + 1) * PEER_NKEYS] * w).astype(bf16))
    a_blk = jnp.concatenate(parts, axis=0)
    o_ref[...] += jnp.dot(vT_ref[...], a_blk, preferred_element_type=f32)


PEER_TOK_TILE = 512
PEER_EXP_TILE = 256


def peer_ffn_pallas(h1, g, w_q, sub1, sub2, u_tab, v_tab):
    f32, bf16 = jnp.float32, jnp.bfloat16
    s, d = h1.shape
    n_exp = u_tab.shape[0]
    tt = min(PEER_TOK_TILE, s)
    nq = PEER_HEADS * PEER_QDIM
    xnT, sT, aux = pl.pallas_call(
        _peer_score_kernel,
        grid=(s // tt,),
        in_specs=[pl.BlockSpec((tt, d), lambda t: (t, 0)),
                  pl.BlockSpec((1, d), lambda t: (0, 0)),
                  pl.BlockSpec((nq, d), lambda t: (0, 0)),
                  pl.BlockSpec((PEER_NKEYS, PEER_HALF), lambda t: (0, 0)),
                  pl.BlockSpec((PEER_NKEYS, PEER_HALF), lambda t: (0, 0))],
        out_specs=[pl.BlockSpec((d, tt), lambda t: (0, t)),
                   pl.BlockSpec((PEER_HEADS, 2 * PEER_NKEYS, tt), lambda t: (0, 0, t)),
                   pl.BlockSpec((PEER_HEADS, AUX_ROWS, tt), lambda t: (0, 0, t))],
        out_shape=[jax.ShapeDtypeStruct((d, s), bf16),
                   jax.ShapeDtypeStruct((PEER_HEADS, 2 * PEER_NKEYS, s), f32),
                   jax.ShapeDtypeStruct((PEER_HEADS, AUX_ROWS, s), f32)],
        scratch_shapes=[pltpu.VMEM((nq, tt), f32), pltpu.VMEM((2 * PEER_TOPK, tt), f32), pltpu.VMEM((1, tt), f32)],
        compiler_params=pltpu.CompilerParams(dimension_semantics=("arbitrary",), vmem_limit_bytes=VMEM_LIMIT),
        name="peer_scores",
    )(h1, g.reshape(1, d), w_q.T.astype(bf16), sub1.astype(bf16), sub2.astype(bf16))
    te = PEER_EXP_TILE
    outT = pl.pallas_call(
        _peer_expert_kernel,
        grid=(s // tt, n_exp // te),
        in_specs=[pl.BlockSpec((d, tt), lambda t, e: (0, t)),
                  pl.BlockSpec((PEER_HEADS, 2 * PEER_NKEYS, tt), lambda t, e: (0, 0, t)),
                  pl.BlockSpec((PEER_HEADS, AUX_ROWS, tt), lambda t, e: (0, 0, t)),
                  pl.BlockSpec((te, d), lambda t, e: (e, 0)),
                  pl.BlockSpec((d, te), lambda t, e: (0, e))],
        out_specs=pl.BlockSpec((d, tt), lambda t, e: (0, t)),
        out_shape=jax.ShapeDtypeStruct((d, s), f32),
        scratch_shapes=[pltpu.VMEM((PEER_HEADS, PEER_NKEYS, tt), f32)] * 3,
        compiler_params=pltpu.CompilerParams(dimension_semantics=("arbitrary", "arbitrary"),
                                             vmem_limit_bytes=VMEM_LIMIT),
        name="peer_experts",
    )(xnT, sT, aux, u_tab.astype(bf16), v_tab.T.astype(bf16))
    return outT.T


def _final_norm_kernel(h_ref, g_ref, o_ref):
    xf = h_ref[...]
    y = xf * lax.rsqrt(jnp.mean(xf * xf, axis=-1, keepdims=True) + NORM_EPS)
    o_ref[...] = y * g_ref[...]


def _final_norm_pallas(h, g):
    b, s, d = h.shape
    tb = 512
    out = pl.pallas_call(
        _final_norm_kernel,
        grid=(b * s // tb,),
        in_specs=[pl.BlockSpec((tb, d), lambda i: (i, 0)), pl.BlockSpec((1, d), lambda i: (0, 0))],
        out_specs=pl.BlockSpec((tb, d), lambda i: (i, 0)),
        out_shape=jax.ShapeDtypeStruct((b * s, d), h.dtype),
        name="final_norm",
    )(h.reshape(b * s, d), g.reshape(1, d))
    return out.reshape(b, s, d)


def kernel(x, attn_norm, w_in, cmp_k_pe, cmp_k_w1, cmp_k_w2, cmp_v_pe, cmp_v_w1, cmp_v_w2, ssm_lam_re, ssm_lam_im, ssm_log_dt, ssm_b_re, ssm_b_im, ssm_c_re, ssm_c_im, ssm_d, ssm_w_glu, w_out, ffn_norm, peer_w_q, peer_subkeys_1, peer_subkeys_2, peer_u, peer_v, final_norm):
    b, s, _ = x.shape
    sizes = [ATTN_WIDTH] + [KV_WIDTH] * 6 + [3 * N_HEADS]
    cuts = [int(c) for c in np.cumsum(sizes)]
    h = x
    for l in range(DEPTH):
        z = rms_norm(h, attn_norm[l])
        proj = z @ w_in[l]
        q, kc, vc, ksl, vsl, kwn, vwn, gates, u = jnp.split(proj, cuts, axis=-1)
        kv_shape = (b, s, N_KV, HEAD_DIM)
        a_out = nsa_attention_pallas(q.reshape(b, s, N_HEADS, HEAD_DIM),
                              kc.reshape(kv_shape), vc.reshape(kv_shape),
                              ksl.reshape(kv_shape), vsl.reshape(kv_shape),
                              kwn.reshape(kv_shape), vwn.reshape(kv_shape), gates,
                              cmp_k_pe[l], cmp_k_w1[l], cmp_k_w2[l],
                              cmp_v_pe[l], cmp_v_w1[l], cmp_v_w2[l])
        s_out = s5_ssm_pallas(u, ssm_lam_re[l], ssm_lam_im[l], ssm_log_dt[l], ssm_b_re[l], ssm_b_im[l],
                       ssm_c_re[l], ssm_c_im[l], ssm_d[l], ssm_w_glu[l])
        h = h + jnp.concatenate([a_out, s_out], axis=-1) @ w_out[l]
        h = h + peer_ffn_pallas(h[0], ffn_norm[l], peer_w_q[l], peer_subkeys_1[l],
                                peer_subkeys_2[l], peer_u[l], peer_v[l])[None]
    return _final_norm_pallas(h, final_norm)
```

```python
import functools
import math
import jax, jax.numpy as jnp
from jax import lax
import numpy as np
from jax.experimental import pallas as pl
from jax.experimental.pallas import tpu as pltpu

D_MODEL = 1024
DEPTH = 1
N_HEADS = 8
HEAD_DIM = 64
N_KV = 2
HPG = N_HEADS // N_KV
ATTN_WIDTH = N_HEADS * HEAD_DIM
SSM_GROUP = 16
SSM_GROUPS = 32
SSM_WIDTH = SSM_GROUP * SSM_GROUPS
SSM_STATE = 64
MIX_WIDTH = ATTN_WIDTH + SSM_WIDTH
KV_WIDTH = N_KV * HEAD_DIM
CMP_LEN = 32
CMP_STRIDE = 16
CMP_HIDDEN = 128
SEL_BLOCK = 64
N_SEL = 16
WINDOW = 512
Q_BLOCK = 128
ROPE_THETA = 500000.0
ROPE_DIM = HEAD_DIM // 4
PEER_HEADS = 8
PEER_NKEYS = 128
PEER_QDIM = 256
PEER_TOPK = 16
PEER_CHUNK = 128
NORM_EPS = 1e-6
NEG_INF = -1e30
FORCE_SCORE = 1e9


def rms_norm(x, g):
    xf = x.astype(jnp.float32)
    y = xf * lax.rsqrt(jnp.mean(xf * xf, axis=-1, keepdims=True) + NORM_EPS)
    return (y * g.astype(jnp.float32)).astype(x.dtype)


def rope(x, pos):
    half = ROPE_DIM // 2
    inv = ROPE_THETA ** (-jnp.arange(half, dtype=jnp.float32) * 2.0 / ROPE_DIM)
    ang = pos.astype(jnp.float32)[:, None] * inv[None, :]
    cos = jnp.cos(ang)[None, :, None, :]
    sin = jnp.sin(ang)[None, :, None, :]
    xr = x[..., :ROPE_DIM].astype(jnp.float32)
    x1, x2 = xr[..., :half], xr[..., half:]
    rot = jnp.concatenate([x1 * cos - x2 * sin, x2 * cos + x1 * sin], axis=-1).astype(x.dtype)
    return jnp.concatenate([rot, x[..., ROPE_DIM:]], axis=-1)


def masked_softmax(s, mask):
    s = jnp.where(mask, s, NEG_INF)
    m = jnp.max(s, axis=-1, keepdims=True)
    e = jnp.where(mask, jnp.exp(s - m), 0.0)
    return e / jnp.maximum(jnp.sum(e, axis=-1, keepdims=True), 1e-30)


def compress(kv, pe, w1, w2):
    b, s = kv.shape[0], kv.shape[1]
    n_cmp = (s - CMP_LEN) // CMP_STRIDE + 1
    idx = jnp.arange(n_cmp)[:, None] * CMP_STRIDE + jnp.arange(CMP_LEN)[None, :]
    blk = kv[:, idx] + pe[None, None, :, None, :]
    blk = blk.transpose(0, 1, 3, 2, 4).reshape(b, n_cmp, N_KV, CMP_LEN * HEAD_DIM)
    return jax.nn.gelu(blk @ w1) @ w2


def nsa_attention(q, kc_raw, vc_raw, ks, vs, kw, vw, gates, pe_k, w1k, w2k, pe_v, w1v, w2v):
    b, s = q.shape[0], q.shape[1]
    pos = jnp.arange(s)
    n_blk = s // Q_BLOCK
    n_selb = s // SEL_BLOCK
    n_sel = min(N_SEL, n_selb)
    q = rope(q, pos) * (HEAD_DIM ** -0.5)
    kc = compress(kc_raw, pe_k, w1k, w2k)
    vc = compress(vc_raw, pe_v, w1v, w2v)
    n_cmp = kc.shape[1]
    cmp_start = jnp.arange(n_cmp) * CMP_STRIDE
    cmp_end = cmp_start + CMP_LEN - 1
    kc = rope(kc, cmp_end).transpose(0, 2, 1, 3)
    vc = vc.transpose(0, 2, 1, 3)
    sel_ids = jnp.arange(n_selb)
    sel_start = sel_ids * SEL_BLOCK
    overlap = ((cmp_start[:, None] < sel_start[None, :] + SEL_BLOCK)
               & (cmp_start[:, None] + CMP_LEN > sel_start[None, :])).astype(jnp.float32)
    ks_blk = rope(ks, pos).reshape(b, n_selb, SEL_BLOCK, N_KV, HEAD_DIM).transpose(0, 3, 1, 2, 4)
    vs_blk = vs.reshape(b, n_selb, SEL_BLOCK, N_KV, HEAD_DIM).transpose(0, 3, 1, 2, 4)
    kw_pad = jnp.pad(rope(kw, pos), ((0, 0), (WINDOW, 0), (0, 0), (0, 0))).transpose(0, 2, 1, 3)
    vw_pad = jnp.pad(vw, ((0, 0), (WINDOW, 0), (0, 0), (0, 0))).transpose(0, 2, 1, 3)
    q_blocks = q.reshape(b, n_blk, Q_BLOCK, N_KV, HPG, HEAD_DIM).transpose(1, 0, 3, 4, 2, 5)
    g_blocks = jax.nn.sigmoid(gates.astype(jnp.float32)).reshape(
        b, n_blk, Q_BLOCK, N_KV, HPG, 3).transpose(1, 0, 3, 4, 2, 5)
    b_ix = jnp.arange(b)[:, None, None, None]
    g_ix = jnp.arange(N_KV)[None, :, None, None]

    def block(args):
        blk, qb, gb = args
        t = blk * Q_BLOCK + jnp.arange(Q_BLOCK)
        s_c = jnp.einsum('bghqd,bgnd->bghqn', qb, kc).astype(jnp.float32)
        p_c = masked_softmax(s_c, cmp_end[None, :] <= t[:, None])
        o_c = jnp.einsum('bghqn,bgnd->bghqd', p_c.astype(vc.dtype), vc)
        imp = jnp.einsum('bghqn,nm->bgqm', p_c, overlap)
        cur = (t // SEL_BLOCK)[:, None]
        valid = sel_start[None, :] <= t[:, None]
        forced = (sel_ids[None, :] == 0) | (sel_ids[None, :] == cur) | (sel_ids[None, :] == cur - 1)
        imp = jnp.where(forced, FORCE_SCORE, jnp.where(valid, imp, NEG_INF))
        _, sidx = lax.top_k(imp, n_sel)
        k_sel = ks_blk[b_ix, g_ix, sidx]
        v_sel = vs_blk[b_ix, g_ix, sidx]
        s_s = jnp.einsum('bghqd,bgqnkd->bghqnk', qb, k_sel).astype(jnp.float32)
        kpos = sidx[..., None] * SEL_BLOCK + jnp.arange(SEL_BLOCK)
        m_s = (kpos <= t[:, None, None])[:, :, None]
        shp = s_s.shape
        p_s = masked_softmax(s_s.reshape(shp[0], shp[1], shp[2], shp[3], -1),
                             m_s.reshape(shp[0], shp[1], 1, shp[3], -1)).reshape(shp)
        o_s = jnp.einsum('bghqnk,bgqnkd->bghqd', p_s.astype(v_sel.dtype), v_sel)
        start = blk * Q_BLOCK
        k_w = lax.dynamic_slice_in_dim(kw_pad, start, Q_BLOCK + WINDOW, axis=2)
        v_w = lax.dynamic_slice_in_dim(vw_pad, start, Q_BLOCK + WINDOW, axis=2)
        kpos_w = start - WINDOW + jnp.arange(Q_BLOCK + WINDOW)
        m_w = ((kpos_w[None, :] >= 0) & (kpos_w[None, :] <= t[:, None])
               & (t[:, None] - kpos_w[None, :] < WINDOW))
        s_w = jnp.einsum('bghqd,bgkd->bghqk', qb, k_w).astype(jnp.float32)
        p_w = masked_softmax(s_w, m_w)
        o_w = jnp.einsum('bghqk,bgkd->bghqd', p_w.astype(v_w.dtype), v_w)
        gb = gb.astype(qb.dtype)
        return gb[..., 0:1] * o_c + gb[..., 1:2] * o_s + gb[..., 2:3] * o_w

    out = lax.map(block, (jnp.arange(n_blk), q_blocks, g_blocks))
    return out.transpose(1, 0, 4, 2, 3, 5).reshape(b, s, ATTN_WIDTH)


KV_TILE = 256
BLOCKS_PER_TILE = KV_TILE // SEL_BLOCK
MASK_BIAS = 1e30
VMEM_LIMIT = 56 * 1024 * 1024


def _nsa_kernel(qT_ref, gate_ref, kc_ref, vcT_ref, ks_ref, vsT_ref, kw_ref, vwT_ref, o_ref,
                ps_ref, sel_ref, *, tq, n_sel):
    f32, bf16 = jnp.float32, jnp.bfloat16
    i = pl.program_id(1)
    qT = qT_ref[0, 0]
    ncol = qT.shape[1]
    ncp = kc_ref.shape[1]
    nsb = sel_ref.shape[0]
    t0 = i * tq
    t_col = t0 + (lax.broadcasted_iota(jnp.int32, (1, ncol), 1) & (tq - 1))
    t_tok = t0 + lax.broadcasted_iota(jnp.int32, (1, tq), 1)

    s = jnp.dot(kc_ref[0], qT, preferred_element_type=f32)
    n_idx = lax.broadcasted_iota(jnp.int32, (ncp, ncol), 0)
    cmask = (n_idx * CMP_STRIDE + (CMP_LEN - 1)) <= t_col
    s = jnp.where(cmask, s, NEG_INF)
    m = jnp.max(s, axis=0, keepdims=True)
    e = jnp.where(cmask, jnp.exp(s - m), 0.0)
    l = jnp.sum(e, axis=0, keepdims=True)
    p = e / jnp.maximum(l, 1e-30)
    o_c = jnp.dot(vcT_ref[0], p.astype(bf16), preferred_element_type=f32)

    psum = p[:, 0:tq]
    for h in range(1, HPG):
        psum = psum + p[:, h * tq:(h + 1) * tq]
    ps_ref[...] = psum
    parts = [ps_ref[pl.ds(k, nsb, stride=4), :] for k in range(4)]
    m_idx = lax.broadcasted_iota(jnp.int32, (nsb, tq), 0)
    prev = jnp.where(m_idx == 0, 0.0, pltpu.roll(parts[3], 1, axis=0))
    imp = parts[0] + parts[1] + parts[2] + parts[3] + prev

    cur = t_tok // SEL_BLOCK
    forced = (m_idx == 0) | (m_idx == cur) | (m_idx == cur - 1)
    valid = m_idx <= cur
    val0 = jnp.where(forced, FORCE_SCORE, jnp.where(valid, imp, NEG_INF))
    m_f = m_idx.astype(f32)

    def pick(_, carry):
        val, sel = carry
        mx = jnp.max(val, axis=0, keepdims=True)
        first = jnp.min(jnp.where(val == mx, m_f, 1e9), axis=0, keepdims=True)
        hit = m_f == first
        return jnp.where(hit, -3e38, val), jnp.where(hit, 1.0, sel)

    _, sel = lax.fori_loop(0, n_sel, pick, (val0, jnp.zeros((nsb, tq), f32)))
    sel_ref[...] = jnp.where(valid, sel, 0.0)

    def sweep(k_ref, vT_ref, j, carry, mask_fn):
        m_i, l_i, acc = carry
        s = jnp.dot(k_ref[0, j], qT, preferred_element_type=f32)
        s = mask_fn(s, j)
        m_new = jnp.maximum(m_i, jnp.max(s, axis=0, keepdims=True))
        alpha = jnp.exp(m_i - m_new)
        pj = jnp.exp(s - m_new)
        l_new = alpha * l_i + jnp.sum(pj, axis=0, keepdims=True)
        acc = alpha * acc + jnp.dot(vT_ref[0, j], pj.astype(bf16), preferred_element_type=f32)
        return m_new, l_new, acc

    def sel_bias(s, j):
        rows = []
        for b in range(BLOCKS_PER_TILE):
            r = sel_ref[pl.ds(j * BLOCKS_PER_TILE + b, 1), :]
            rows.append(jnp.broadcast_to((r - 1.0) * MASK_BIAS, (SEL_BLOCK, tq)))
        bias = jnp.concatenate(rows, axis=0)
        return s + jnp.concatenate([bias] * HPG, axis=1)

    def kpos_of(j):
        return j * KV_TILE + lax.broadcasted_iota(jnp.int32, (KV_TILE, ncol), 0)

    def sel_diag(s, j):
        return jnp.where(kpos_of(j) <= t_col, sel_bias(s, j), NEG_INF)

    def win_mask(s, j):
        kpos = kpos_of(j)
        return jnp.where((kpos <= t_col) & (t_col - kpos < WINDOW), s, NEG_INF)

    init = (jnp.full((1, ncol), NEG_INF, f32), jnp.zeros((1, ncol), f32), jnp.zeros((HEAD_DIM, ncol), f32))
    jd = t0 // KV_TILE
    carry = lax.fori_loop(0, jd, lambda j, c: sweep(ks_ref, vsT_ref, j, c, sel_bias), init)
    _, l_s, acc_s = sweep(ks_ref, vsT_ref, jd, carry, sel_diag)
    jw0 = jnp.maximum(t0 - (WINDOW - 1), 0) // KV_TILE
    _, l_w, acc_w = lax.fori_loop(jw0, jd + 1, lambda j, c: sweep(kw_ref, vwT_ref, j, c, win_mask), init)

    g = jax.nn.sigmoid(gate_ref[0, 0])
    o_ref[0, 0] = (g[0:1] * o_c + g[1:2] * (acc_s / jnp.maximum(l_s, 1e-30))
                   + g[2:3] * (acc_w / jnp.maximum(l_w, 1e-30)))


def _nsa_pallas(qT, gateT, kc, vcT, ks, vsT, kw, vwT, *, tq, n_sel):
    g_, nq, _, ncol = qT.shape
    ncp = kc.shape[1]
    nt = ks.shape[1]
    nsb = nt * BLOCKS_PER_TILE
    per_q = lambda g, i: (g, i, 0, 0)
    per_g3 = lambda g, i: (g, 0, 0)
    per_g4 = lambda g, i: (g, 0, 0, 0)
    return pl.pallas_call(
        functools.partial(_nsa_kernel, tq=tq, n_sel=n_sel),
        grid=(g_, nq),
        in_specs=[pl.BlockSpec((1, 1, HEAD_DIM, ncol), per_q),
                  pl.BlockSpec((1, 1, 3, ncol), per_q),
                  pl.BlockSpec((1, ncp, HEAD_DIM), per_g3),
                  pl.BlockSpec((1, HEAD_DIM, ncp), per_g3),
                  pl.BlockSpec((1, nt, KV_TILE, HEAD_DIM), per_g4),
                  pl.BlockSpec((1, nt, HEAD_DIM, KV_TILE), per_g4),
                  pl.BlockSpec((1, nt, KV_TILE, HEAD_DIM), per_g4),
                  pl.BlockSpec((1, nt, HEAD_DIM, KV_TILE), per_g4)],
        out_specs=pl.BlockSpec((1, 1, HEAD_DIM, ncol), per_q),
        out_shape=jax.ShapeDtypeStruct((g_, nq, HEAD_DIM, ncol), jnp.float32),
        scratch_shapes=[pltpu.VMEM((ncp, tq), jnp.float32), pltpu.VMEM((nsb, tq), jnp.float32)],
        compiler_params=pltpu.CompilerParams(dimension_semantics=("arbitrary", "arbitrary"),
                                             vmem_limit_bytes=VMEM_LIMIT),
        name="nsa_attention",
    )(qT, gateT, kc, vcT, ks, vsT, kw, vwT)


def nsa_attention_pallas(q, kc_raw, vc_raw, ks, vs, kw, vw, gates, pe_k, w1k, w2k, pe_v, w1v, w2v):
    b, s = q.shape[0], q.shape[1]
    assert b == 1 and s % KV_TILE == 0
    bf16 = jnp.bfloat16
    tq = Q_BLOCK
    nq, nt = s // tq, s // KV_TILE
    pos = jnp.arange(s)
    n_sel = min(N_SEL, s // SEL_BLOCK)
    q = rope(q, pos) * (HEAD_DIM ** -0.5)
    kc = compress(kc_raw, pe_k, w1k, w2k)
    vc = compress(vc_raw, pe_v, w1v, w2v)
    n_cmp = kc.shape[1]
    ncp = s // CMP_STRIDE
    kc = rope(kc, jnp.arange(n_cmp) * CMP_STRIDE + CMP_LEN - 1)
    pad = ((0, ncp - n_cmp), (0, 0), (0, 0))
    kc_g = jnp.pad(kc[0], pad).transpose(1, 0, 2).astype(bf16)
    vcT = jnp.pad(vc[0], pad).transpose(1, 2, 0).astype(bf16)
    tile_k = lambda k: k[0].transpose(1, 0, 2).reshape(N_KV, nt, KV_TILE, HEAD_DIM).astype(bf16)
    tile_vT = lambda v: v[0].transpose(1, 2, 0).reshape(N_KV, HEAD_DIM, nt, KV_TILE).transpose(0, 2, 1, 3).astype(bf16)
    qT = q[0].reshape(nq, tq, N_KV, HPG, HEAD_DIM).transpose(2, 0, 4, 3, 1).reshape(N_KV, nq, HEAD_DIM, HPG * tq)
    gateT = gates[0].reshape(nq, tq, N_KV, HPG, 3).transpose(2, 0, 4, 3, 1).reshape(N_KV, nq, 3, HPG * tq)
    oT = _nsa_pallas(qT.astype(bf16), gateT.astype(jnp.float32), kc_g, vcT,
                     tile_k(rope(ks, pos)), tile_vT(vs), tile_k(rope(kw, pos)), tile_vT(vw), tq=tq, n_sel=n_sel)
    out = oT.reshape(N_KV, nq, HEAD_DIM, HPG, tq).transpose(1, 4, 0, 3, 2)
    return out.reshape(b, s, ATTN_WIDTH)


def s5_ssm(u, lam_re, lam_im, log_dt, b_re, b_im, c_re, c_im, d_skip, w_glu):
    b, s = u.shape[0], u.shape[1]
    f32 = jnp.float32
    lam = lax.complex(lam_re.astype(f32), lam_im.astype(f32))
    dt = jnp.exp(log_dt.astype(f32))[:, None]
    lam_bar = jnp.exp(lam * dt)
    bmat = lax.complex(b_re.astype(f32), b_im.astype(f32))
    b_bar = ((lam_bar - 1.0) / lam)[..., None] * bmat
    cmat = lax.complex(c_re.astype(f32), c_im.astype(f32))
    uf = u.astype(f32)
    ug = uf.reshape(b, s, SSM_GROUPS, SSM_GROUP).astype(jnp.complex64)
    bu = jnp.einsum('gpc,bsgc->bsgp', b_bar, ug)
    a = jnp.broadcast_to(lam_bar, bu.shape)

    def combine(e1, e2):
        a1, b1 = e1
        a2, b2 = e2
        return (a1 * a2, a2 * b1 + b2)

    _, states = lax.associative_scan(combine, (a, bu), axis=1)
    y = jnp.real(jnp.einsum('gcp,bsgp->bsgc', cmat, states)).reshape(b, s, SSM_WIDTH)
    y = jax.nn.gelu(y + d_skip.astype(f32) * uf).astype(u.dtype)
    ab = y @ w_glu
    return ab[..., :SSM_WIDTH] * jax.nn.sigmoid(ab[..., SSM_WIDTH:])


def peer_ffn(x, w_q, sub1, sub2, u_tab, v_tab):
    b, s, d = x.shape
    n_ch = s // PEER_CHUNK
    half = PEER_QDIM // 2
    xc = x.reshape(b, n_ch, PEER_CHUNK, d).transpose(1, 0, 2, 3)

    def chunk(xb):
        q = (xb @ w_q).reshape(b, PEER_CHUNK, PEER_HEADS, PEER_QDIM)
        s1 = jnp.einsum('bchd,kd->bchk', q[..., :half], sub1).astype(jnp.float32)
        s2 = jnp.einsum('bchd,kd->bchk', q[..., half:], sub2).astype(jnp.float32)
        v1, i1 = lax.top_k(s1, PEER_TOPK)
        v2, i2 = lax.top_k(s2, PEER_TOPK)
        cand = (v1[..., :, None] + v2[..., None, :]).reshape(b, PEER_CHUNK, PEER_HEADS, PEER_TOPK * PEER_TOPK)
        cidx = (i1[..., :, None] * PEER_NKEYS + i2[..., None, :]).reshape(b, PEER_CHUNK, PEER_HEADS, PEER_TOPK * PEER_TOPK)
        top, sel = lax.top_k(cand, PEER_TOPK)
        eidx = jnp.take_along_axis(cidx, sel, axis=-1)
        gate = jax.nn.softmax(top, axis=-1)
        u_sel = u_tab[eidx]
        v_sel = v_tab[eidx]
        hid = jnp.einsum('bcd,bchkd->bchk', xb, u_sel).astype(jnp.float32)
        act = (jax.nn.gelu(hid) * gate).astype(xb.dtype)
        return jnp.einsum('bchk,bchkd->bcd', act, v_sel)

    out = lax.map(chunk, xc)
    return out.transpose(1, 0, 2, 3).reshape(b, s, d)


SSM_CHUNK = 16
SSM_ROW = SSM_CHUNK * SSM_GROUP
HI = lax.Precision.HIGHEST


def _cplx_pow(lr_dt, li_dt, n):
    mag = jnp.exp(n * lr_dt)
    return mag * jnp.cos(n * li_dt), mag * jnp.sin(n * li_dt)


def _ssm_kernel(u_ref, ldt_ref, lr_row_ref, li_row_ref, lr_col_ref, li_col_ref,
                bre_t_ref, bim_t_ref, cre_t_ref, cim_t_ref, y_ref):
    f32, bf16 = jnp.float32, jnp.bfloat16
    L, C, P = SSM_CHUNK, SSM_GROUP, SSM_STATE
    nc = u_ref.shape[1]
    dt = jnp.exp(ldt_ref[0])

    def zoh(lr, li):
        ar, ai = _cplx_pow(lr * dt, li * dt, 1.0)
        den = lr * lr + li * li
        return ar, ai, ((ar - 1.0) * lr + ai * li) / den, (ai * lr - (ar - 1.0) * li) / den

    lr_r, li_r = lr_row_ref[0], li_row_ref[0]
    _, _, fr_r, fi_r = zoh(lr_r, li_r)
    bbt_re = fr_r * bre_t_ref[0] - fi_r * bim_t_ref[0]
    bbt_im = fr_r * bim_t_ref[0] + fi_r * bre_t_ref[0]
    lr_c, li_c = lr_col_ref[0], li_col_ref[0]
    lane = lax.broadcasted_iota(jnp.int32, (1, L * C), 1)
    step = (lane // C).astype(f32)
    tile = (lax.broadcasted_iota(jnp.int32, (C, L * C), 0) == (lane % C)).astype(f32)
    cmt_re = jnp.dot(cre_t_ref[0], tile, precision=HI, preferred_element_type=f32)
    cmt_im = jnp.dot(cim_t_ref[0], tile, precision=HI, preferred_element_type=f32)

    def c_times_pow(n):
        pr, pi = _cplx_pow(lr_c * dt, li_c * dt, n)
        return pr * cmt_re - pi * cmt_im, pr * cmt_im + pi * cmt_re

    w_re, w_im = c_times_pow(step)
    taps = (jnp.dot(bbt_re, w_re, precision=HI, preferred_element_type=f32)
            - jnp.dot(bbt_im, w_im, precision=HI, preferred_element_type=f32))
    slabs = []
    for r in range(L):
        sl = taps if r == 0 else jnp.where(lane >= r * C, pltpu.roll(taps, r * C, axis=1), 0.0)
        slabs.append(sl)
    t_mat = jnp.concatenate(slabs, axis=0).astype(bf16)

    row_step = (lax.broadcasted_iota(jnp.int32, (L * C, 1), 0) // C).astype(f32)
    pr, pi = _cplx_pow(lr_r * dt, li_r * dt, (L - 1.0) - row_step)
    bt_re = jnp.concatenate([bbt_re] * L, axis=0)
    bt_im = jnp.concatenate([bbt_im] * L, axis=0)
    m_re = (pr * bt_re - pi * bt_im).astype(bf16)
    m_im = (pr * bt_im + pi * bt_re).astype(bf16)
    n_re, n_im = c_times_pow(step + 1.0)

    u = u_ref[0]
    e_re = jnp.dot(u, m_re, preferred_element_type=f32)
    e_im = jnp.dot(u, m_im, preferred_element_type=f32)
    k_idx = lax.broadcasted_iota(jnp.int32, (nc, P), 0)
    qr, qi = _cplx_pow(lr_r * dt, li_r * dt, float(L))
    d = 1
    while d < nc:
        sr = jnp.where(k_idx >= d, pltpu.roll(e_re, d, axis=0), 0.0)
        si = jnp.where(k_idx >= d, pltpu.roll(e_im, d, axis=0), 0.0)
        e_re, e_im = e_re + qr * sr - qi * si, e_im + qr * si + qi * sr
        qr, qi = qr * qr - qi * qi, 2.0 * qr * qi
        d *= 2
    x_re = jnp.where(k_idx >= 1, pltpu.roll(e_re, 1, axis=0), 0.0)
    x_im = jnp.where(k_idx >= 1, pltpu.roll(e_im, 1, axis=0), 0.0)
    y = jnp.dot(u, t_mat, preferred_element_type=f32)
    y = y + jnp.dot(x_re.astype(bf16), n_re.astype(bf16), preferred_element_type=f32)
    y = y - jnp.dot(x_im.astype(bf16), n_im.astype(bf16), preferred_element_type=f32)
    y_ref[0] = y


def ssm_scan_pallas(u, lam_re, lam_im, log_dt, b_re, b_im, c_re, c_im):
    f32, bf16 = jnp.float32, jnp.bfloat16
    s = u.shape[0]
    G, C, P, L = SSM_GROUPS, SSM_GROUP, SSM_STATE, SSM_CHUNK
    nc = s // L
    ug = u.reshape(nc, L, G, C).transpose(2, 0, 1, 3).reshape(G, nc, L * C).astype(bf16)
    per_g = lambda g: (g, 0, 0)
    spec = lambda a, b: pl.BlockSpec((1, a, b), per_g)
    y = pl.pallas_call(
        _ssm_kernel,
        grid=(G,),
        in_specs=[spec(nc, L * C), spec(1, 1), spec(1, P), spec(1, P), spec(P, 1), spec(P, 1),
                  spec(C, P), spec(C, P), spec(P, C), spec(P, C)],
        out_specs=spec(nc, L * C),
        out_shape=jax.ShapeDtypeStruct((G, nc, L * C), f32),
        compiler_params=pltpu.CompilerParams(dimension_semantics=("arbitrary",), vmem_limit_bytes=VMEM_LIMIT),
        name="ssm_scan",
    )(ug, log_dt.reshape(G, 1, 1), lam_re.reshape(G, 1, P), lam_im.reshape(G, 1, P),
      lam_re.reshape(G, P, 1), lam_im.reshape(G, P, 1),
      b_re.transpose(0, 2, 1), b_im.transpose(0, 2, 1), c_re.transpose(0, 2, 1), c_im.transpose(0, 2, 1))
    return y.reshape(G, nc, L, C).transpose(1, 2, 0, 3).reshape(s, G * C)


def s5_ssm_pallas(u, lam_re, lam_im, log_dt, b_re, b_im, c_re, c_im, d_skip, w_glu):
    y = ssm_scan_pallas(u[0], lam_re, lam_im, log_dt, b_re, b_im, c_re, c_im)[None]
    y = jax.nn.gelu(y + d_skip.astype(jnp.float32) * u)
    ab = y @ w_glu
    return ab[..., :SSM_WIDTH] * jax.nn.sigmoid(ab[..., SSM_WIDTH:])


PEER_HALF = PEER_QDIM // 2
AUX_TAU, AUX_M1, AUX_M2, AUX_Z, AUX_ROWS = PEER_TOPK, 2 * PEER_TOPK, 2 * PEER_TOPK + 1, 2 * PEER_TOPK + 2, 40
BIG = 3e38


def _top_values(val, k, store_row):
    r_f = lax.broadcasted_iota(jnp.int32, val.shape, 0).astype(jnp.float32)

    def body(a, v):
        mx = jnp.max(v, axis=0, keepdims=True)
        first = jnp.min(jnp.where(v == mx, r_f, 1e9), axis=0, keepdims=True)
        store_row(a, mx)
        return jnp.where(r_f == first, -BIG, v)

    lax.fori_loop(0, k, body, val)


def _peer_score_kernel(h_ref, g_ref, wqT_ref, sub1_ref, sub2_ref, xnT_ref, sT_ref, aux_ref, q_scr, v_scr, c_scr):
    f32, bf16 = jnp.float32, jnp.bfloat16
    k = PEER_TOPK
    x = h_ref[...]
    xn = x * lax.rsqrt(jnp.mean(x * x, axis=-1, keepdims=True) + NORM_EPS) * g_ref[...]
    xnb = xn.astype(bf16)
    xnT_ref[...] = xn.T.astype(bf16)
    q_scr[...] = lax.dot_general(wqT_ref[...], xnb, (((1,), (1,)), ((), ())), preferred_element_type=f32)
    tt = x.shape[0]

    def head(h, _):
        base = pl.multiple_of(h * PEER_QDIM, PEER_QDIM)
        s1 = jnp.dot(sub1_ref[...], q_scr[pl.ds(base, PEER_HALF), :].astype(bf16), preferred_element_type=f32)
        s2 = jnp.dot(sub2_ref[...], q_scr[pl.ds(base + PEER_HALF, PEER_HALF), :].astype(bf16), preferred_element_type=f32)
        sT_ref[h, 0:PEER_NKEYS, :] = s1
        sT_ref[h, PEER_NKEYS:2 * PEER_NKEYS, :] = s2

        def put(off):
            def store(a, row):
                v_scr[pl.ds(off + a, 1), :] = row
            return store

        _top_values(s1, k, put(0))
        _top_values(s2, k, put(k))
        v1 = v_scr[0:k, :]
        v2 = v_scr[k:2 * k, :]
        cand = [v1[0:1] + v2] + [v1[a:a + 1] + v2[0:8] for a in range(1, 8)] + [v1[8:k] + v2[0:1]]
        cand = jnp.concatenate(cand, axis=0)

        def keep_last(a, row):
            c_scr[...] = row

        _top_values(cand, k, keep_last)
        thr = c_scr[...]
        m1, m2 = v1[0:1], v2[0:1]
        z = jnp.sum(jnp.where(cand >= thr, jnp.exp(cand - (m1 + m2)), 0.0), axis=0, keepdims=True)
        taus = [jnp.min(jnp.where(v1[a:a + 1] + v2 >= thr, v2, BIG), axis=0, keepdims=True) for a in range(k)]
        pad = jnp.zeros((AUX_ROWS - AUX_Z - 1, tt), f32)
        aux_ref[h] = jnp.concatenate([v1] + taus + [m1, m2, z, pad], axis=0)
        return 0

    lax.fori_loop(0, PEER_HEADS, head, 0)


def _peer_expert_kernel(xnT_ref, sT_ref, aux_ref, u_ref, vT_ref, o_ref, e1_scr, e2_scr, tau_scr):
    f32, bf16 = jnp.float32, jnp.bfloat16
    e = pl.program_id(1)
    tt = xnT_ref.shape[1]
    n_i = u_ref.shape[0] // PEER_NKEYS

    @pl.when(e == 0)
    def _():
        o_ref[...] = jnp.zeros_like(o_ref)

        def head(h, _):
            aux = aux_ref[h]
            s1 = sT_ref[h, 0:PEER_NKEYS, :]
            s2 = sT_ref[h, PEER_NKEYS:2 * PEER_NKEYS, :]
            e1_scr[h] = jnp.exp(s1 - aux[AUX_M1:AUX_M1 + 1])
            e2_scr[h] = jnp.exp(s2 - aux[AUX_M2:AUX_M2 + 1]) / aux[AUX_Z:AUX_Z + 1]
            tau = jnp.full((PEER_NKEYS, tt), BIG, f32)
            for a in range(PEER_TOPK):
                tau = jnp.where(s1 == aux[a:a + 1], aux[AUX_TAU + a:AUX_TAU + a + 1], tau)
            tau_scr[h] = tau
            return 0

        lax.fori_loop(0, PEER_HEADS, head, 0)

    hid = jnp.dot(u_ref[...], xnT_ref[...], preferred_element_type=f32)
    act = jax.nn.gelu(hid)
    parts = []
    for ii in range(n_i):
        i = e * n_i + ii

        w = jnp.zeros((PEER_NKEYS, tt), f32)
        for h in range(PEER_HEADS):
            tau_row = tau_scr[h, pl.ds(i, 1), :]
            e1_row = e1_scr[h, pl.ds(i, 1), :]
            w = w + jnp.where(sT_ref[h, PEER_NKEYS:2 * PEER_NKEYS, :] >= tau_row, e2_scr[h], 0.0) * e1_row
        parts.append((act[ii * PEER_NKEYS:(ii + 1) * PEER_NKEYS] * w).astype(bf16))
    a_blk = jnp.concatenate(parts, axis=0)
    o_ref[...] += jnp.dot(vT_ref[...], a_blk, preferred_element_type=f32)


PEER_TOK_TILE = 512
PEER_EXP_TILE = 256


def peer_ffn_pallas(h1, g, w_q, sub1, sub2, u_tab, v_tab):
    f32, bf16 = jnp.float32, jnp.bfloat16
    s, d = h1.shape
    n_exp = u_tab.shape[0]
    tt = min(PEER_TOK_TILE, s)
    nq = PEER_HEADS * PEER_QDIM
    xnT, sT, aux = pl.pallas_call(
        _peer_score_kernel,
        grid=(s // tt,),
        in_specs=[pl.BlockSpec((tt, d), lambda t: (t, 0)),
                  pl.BlockSpec((1, d), lambda t: (0, 0)),
                  pl.BlockSpec((nq, d), lambda t: (0, 0)),
                  pl.BlockSpec((PEER_NKEYS, PEER_HALF), lambda t: (0, 0)),
                  pl.BlockSpec((PEER_NKEYS, PEER_HALF), lambda t: (0, 0))],
        out_specs=[pl.BlockSpec((d, tt), lambda t: (0, t)),
                   pl.BlockSpec((PEER_HEADS, 2 * PEER_NKEYS, tt), lambda t: (0, 0, t)),
                   pl.BlockSpec((PEER_HEADS, AUX_ROWS, tt), lambda t: (0, 0, t))],
        out_shape=[jax.ShapeDtypeStruct((d, s), bf16),
                   jax.ShapeDtypeStruct((PEER_HEADS, 2 * PEER_NKEYS, s), f32),
                   jax.ShapeDtypeStruct((PEER_HEADS, AUX_ROWS, s), f32)],
        scratch_shapes=[pltpu.VMEM((nq, tt), f32), pltpu.VMEM((2 * PEER_TOPK, tt), f32), pltpu.VMEM((1, tt), f32)],
        compiler_params=pltpu.CompilerParams(dimension_semantics=("arbitrary",), vmem_limit_bytes=VMEM_LIMIT),
        name="peer_scores",
    )(h1, g.reshape(1, d), w_q.T.astype(bf16), sub1.astype(bf16), sub2.astype(bf16))
    te = PEER_EXP_TILE
    outT = pl.pallas_call(
        _peer_expert_kernel,
        grid=(s // tt, n_exp // te),
        in_specs=[pl.BlockSpec((d, tt), lambda t, e: (0, t)),
                  pl.BlockSpec((PEER_HEADS, 2 * PEER_NKEYS, tt), lambda t, e: (0, 0, t)),
                  pl.BlockSpec((PEER_HEADS, AUX_ROWS, tt), lambda t, e: (0, 0, t)),
                  pl.BlockSpec((te, d), lambda t, e: (e, 0)),
                  pl.BlockSpec((d, te), lambda t, e: (0, e))],
        out_specs=pl.BlockSpec((d, tt), lambda t, e: (0, t)),
        out_shape=jax.ShapeDtypeStruct((d, s), f32),
        scratch_shapes=[pltpu.VMEM((PEER_HEADS, PEER_NKEYS, tt), f32)] * 3,
        compiler_params=pltpu.CompilerParams(dimension_semantics=("arbitrary", "arbitrary"),
                                             vmem_limit_bytes=VMEM_LIMIT),
        name="peer_experts",
    )(xnT, sT, aux, u_tab.astype(bf16), v_tab.T.astype(bf16))
    return outT


TOK_TILE = 512
ROPE_COLS = ATTN_WIDTH + 2 * KV_WIDTH
PROJ_KV = 6 * KV_WIDTH
GATE_PAD = 128


def _rope_tables(pos, width):
    half = ROPE_DIM // 2
    inv = ROPE_THETA ** (-jnp.arange(half, dtype=jnp.float32) * 2.0 / ROPE_DIM)
    ang = pos.astype(jnp.float32)[:, None] * inv[None, :]
    n = pos.shape[0]
    cos = jnp.concatenate([jnp.cos(ang), jnp.cos(ang), jnp.ones((n, HEAD_DIM - ROPE_DIM), jnp.float32)], axis=1)
    sin = jnp.concatenate([jnp.sin(ang), jnp.sin(ang), jnp.zeros((n, HEAD_DIM - ROPE_DIM), jnp.float32)], axis=1)
    reps = width // HEAD_DIM
    return jnp.tile(cos, (1, reps)), jnp.tile(sin, (1, reps))


def _rope_partner(w):
    shp = w.shape
    wh = w.reshape(shp[:-1] + (shp[-1] // HEAD_DIM, HEAD_DIM))
    half = ROPE_DIM // 2
    out = jnp.concatenate([-wh[..., half:ROPE_DIM], wh[..., :half], jnp.zeros_like(wh[..., ROPE_DIM:])], axis=-1)
    return out.reshape(shp)


def _in_proj_kernel(x_ref, g_ref, w_ref, wp_ref, cos_ref, sin_ref, q_ref, kv_ref, u_ref, gate_ref):
    f32, bf16 = jnp.float32, jnp.bfloat16
    x = x_ref[...]
    z = (x * lax.rsqrt(jnp.mean(x * x, axis=-1, keepdims=True) + NORM_EPS) * g_ref[...]).astype(bf16)
    proj = jnp.dot(z, w_ref[...], preferred_element_type=f32)
    part = jnp.dot(z, wp_ref[...], preferred_element_type=f32)
    cos2, sin2 = cos_ref[...], sin_ref[...]
    rot = lambda v, p: v * jnp.concatenate([cos2] * (v.shape[1] // 128), axis=1) \
        + p * jnp.concatenate([sin2] * (v.shape[1] // 128), axis=1)
    a, k = ATTN_WIDTH, KV_WIDTH
    q_ref[...] = (rot(proj[:, :a], part[:, :a]) * (HEAD_DIM ** -0.5)).astype(bf16)
    kv = proj[:, a:a + PROJ_KV]
    ks = rot(kv[:, 2 * k:3 * k], part[:, a:a + k])
    kw = rot(kv[:, 4 * k:5 * k], part[:, a + k:a + 2 * k])
    kv_ref[...] = jnp.concatenate([kv[:, :2 * k], ks, kv[:, 3 * k:4 * k], kw, kv[:, 5 * k:]], axis=1).astype(bf16)
    u_ref[...] = proj[:, a + PROJ_KV:a + PROJ_KV + SSM_WIDTH]
    gate_ref[...] = proj[:, a + PROJ_KV + SSM_WIDTH:]


def in_proj_pallas(x, g, w_in):
    f32, bf16 = jnp.float32, jnp.bfloat16
    s, d = x.shape
    a, k = ATTN_WIDTH, KV_WIDTH
    c_kv, c_gate, c_u = a, a + PROJ_KV, a + PROJ_KV + 3 * N_HEADS
    w = jnp.concatenate([w_in[:, :c_gate], w_in[:, c_u:], w_in[:, c_gate:c_u],
                         jnp.zeros((d, GATE_PAD - 3 * N_HEADS), f32)], axis=1)
    wp = _rope_partner(jnp.concatenate([w_in[:, :a], w_in[:, c_kv + 2 * k:c_kv + 3 * k],
                                        w_in[:, c_kv + 4 * k:c_kv + 5 * k]], axis=1))
    cos2, sin2 = _rope_tables(jnp.arange(s), 128)
    tt = min(TOK_TILE, s)
    n_all = w.shape[1]
    tok = lambda n: pl.BlockSpec((tt, n), lambda t: (t, 0))
    full = lambda r, n: pl.BlockSpec((r, n), lambda t: (0, 0))
    return pl.pallas_call(
        _in_proj_kernel,
        grid=(s // tt,),
        in_specs=[tok(d), full(1, d), full(d, n_all), full(d, ROPE_COLS), tok(128), tok(128)],
        out_specs=[tok(a), tok(PROJ_KV), tok(SSM_WIDTH), tok(GATE_PAD)],
        out_shape=[jax.ShapeDtypeStruct((s, a), bf16), jax.ShapeDtypeStruct((s, PROJ_KV), bf16),
                   jax.ShapeDtypeStruct((s, SSM_WIDTH), f32), jax.ShapeDtypeStruct((s, GATE_PAD), f32)],
        compiler_params=pltpu.CompilerParams(dimension_semantics=("arbitrary",), vmem_limit_bytes=VMEM_LIMIT),
        name="in_proj",
    )(x, g.reshape(1, d), w.astype(bf16), wp.astype(bf16), cos2, sin2)


def _compress_kernel(rk_ref, rv_ref, pek_ref, pev_ref, w1k_ref, w1v_ref, w2k_ref, w2vT_ref, cos_ref, sin_ref,
                     swap_ref, kc_ref, vcT_ref):
    f32, bf16 = jnp.float32, jnp.bfloat16
    ncp = rk_ref.shape[1]
    half = CMP_STRIDE * HEAD_DIM
    rk = rk_ref[0]
    first = jnp.dot(rk, w1k_ref[:half, :], preferred_element_type=f32)
    second = jnp.dot(rk, w1k_ref[half:, :], preferred_element_type=f32)
    pe_term = jnp.dot(pek_ref[...], w1k_ref[...], preferred_element_type=f32)
    hid = jax.nn.gelu(first + pltpu.roll(second, ncp - 1, axis=0) + pe_term)
    kc = jnp.dot(hid.astype(bf16), w2k_ref[...], preferred_element_type=f32)
    partner = jnp.dot(kc, swap_ref[...], precision=HI, preferred_element_type=f32)
    kc_ref[0] = (kc * cos_ref[...] + partner * sin_ref[...]).astype(bf16)
    rv = rv_ref[0]
    nt = (((1,), (1,)), ((), ()))
    first_t = lax.dot_general(w1v_ref[:, :half], rv, nt, preferred_element_type=f32)
    second_t = lax.dot_general(w1v_ref[:, half:], rv, nt, preferred_element_type=f32)
    pe_t = jnp.sum(w1v_ref[...].astype(f32) * pev_ref[...].astype(f32), axis=1, keepdims=True)
    hid_t = jax.nn.gelu(first_t + pltpu.roll(second_t, ncp - 1, axis=1) + pe_t)
    vcT_ref[0] = jnp.dot(w2vT_ref[...], hid_t.astype(bf16), preferred_element_type=f32).astype(bf16)


def compress_pallas(kc_raw, vc_raw, pe_k, w1k, w2k, pe_v, w1v, w2v):
    f32, bf16 = jnp.float32, jnp.bfloat16
    s = kc_raw.shape[0]
    ncp = s // CMP_STRIDE
    wide = CMP_STRIDE * HEAD_DIM
    regroup = lambda v: v.reshape(ncp, CMP_STRIDE, N_KV, HEAD_DIM).transpose(2, 0, 1, 3).reshape(N_KV, ncp, wide)
    cos, sin = _rope_tables(jnp.arange(ncp) * CMP_STRIDE + CMP_LEN - 1, HEAD_DIM)
    swap = _rope_partner(jnp.eye(HEAD_DIM, dtype=f32))
    per_g = lambda g: (g, 0, 0)
    full = lambda r, n: pl.BlockSpec((r, n), lambda g: (0, 0))
    return pl.pallas_call(
        _compress_kernel,
        grid=(N_KV,),
        in_specs=[pl.BlockSpec((1, ncp, wide), per_g), pl.BlockSpec((1, ncp, wide), per_g),
                  full(1, 2 * wide), full(1, 2 * wide), full(2 * wide, CMP_HIDDEN), full(CMP_HIDDEN, 2 * wide),
                  full(CMP_HIDDEN, HEAD_DIM), full(HEAD_DIM, CMP_HIDDEN), full(ncp, HEAD_DIM), full(ncp, HEAD_DIM),
                  full(HEAD_DIM, HEAD_DIM)],
        out_specs=[pl.BlockSpec((1, ncp, HEAD_DIM), per_g), pl.BlockSpec((1, HEAD_DIM, ncp), per_g)],
        out_shape=[jax.ShapeDtypeStruct((N_KV, ncp, HEAD_DIM), bf16), jax.ShapeDtypeStruct((N_KV, HEAD_DIM, ncp), bf16)],
        compiler_params=pltpu.CompilerParams(dimension_semantics=("arbitrary",), vmem_limit_bytes=VMEM_LIMIT),
        name="compress",
    )(regroup(kc_raw), regroup(vc_raw), pe_k.reshape(1, 2 * wide).astype(bf16), pe_v.reshape(1, 2 * wide).astype(bf16),
      w1k.astype(bf16), w1v.T.astype(bf16), w2k.astype(bf16), w2v.T.astype(bf16), cos, sin, swap)


def _mix_kernel(h_ref, a_ref, y_ref, u_ref, d_ref, wglu_ref, wout_ref, o_ref):
    f32, bf16 = jnp.float32, jnp.bfloat16
    y = jax.nn.gelu(y_ref[...] + d_ref[...] * u_ref[...]).astype(bf16)
    ab = jnp.dot(y, wglu_ref[...], preferred_element_type=f32)
    s_out = ab[:, :SSM_WIDTH] * jax.nn.sigmoid(ab[:, SSM_WIDTH:])
    mix = jnp.concatenate([a_ref[...].astype(bf16), s_out.astype(bf16)], axis=1)
    o_ref[...] = h_ref[...] + jnp.dot(mix, wout_ref[...], preferred_element_type=f32)


def mix_pallas(h, a_out, y, u, d_skip, w_glu, w_out):
    f32, bf16 = jnp.float32, jnp.bfloat16
    s, d = h.shape
    tt = min(TOK_TILE, s)
    tok = lambda n: pl.BlockSpec((tt, n), lambda t: (t, 0))
    full = lambda r, n: pl.BlockSpec((r, n), lambda t: (0, 0))
    return pl.pallas_call(
        _mix_kernel,
        grid=(s // tt,),
        in_specs=[tok(d), tok(ATTN_WIDTH), tok(SSM_WIDTH), tok(SSM_WIDTH), full(1, SSM_WIDTH),
                  full(SSM_WIDTH, 2 * SSM_WIDTH), full(MIX_WIDTH, d)],
        out_specs=tok(d),
        out_shape=jax.ShapeDtypeStruct((s, d), f32),
        compiler_params=pltpu.CompilerParams(dimension_semantics=("arbitrary",), vmem_limit_bytes=VMEM_LIMIT),
        name="mix_out",
    )(h, a_out, y, u, d_skip.reshape(1, SSM_WIDTH), w_glu.astype(bf16), w_out.astype(bf16))


def _final_kernel(h_ref, pT_ref, g_ref, o_ref):
    xf = h_ref[...] + pT_ref[...].T
    y = xf * lax.rsqrt(jnp.mean(xf * xf, axis=-1, keepdims=True) + NORM_EPS)
    o_ref[...] = y * g_ref[...]


def final_pallas(h, peer_t, g):
    s, d = h.shape
    tt = min(TOK_TILE, s)
    return pl.pallas_call(
        _final_kernel,
        grid=(s // tt,),
        in_specs=[pl.BlockSpec((tt, d), lambda t: (t, 0)), pl.BlockSpec((d, tt), lambda t: (0, t)),
                  pl.BlockSpec((1, d), lambda t: (0, 0))],
        out_specs=pl.BlockSpec((tt, d), lambda t: (t, 0)),
        out_shape=jax.ShapeDtypeStruct((s, d), jnp.float32),
        compiler_params=pltpu.CompilerParams(dimension_semantics=("arbitrary",), vmem_limit_bytes=VMEM_LIMIT),
        name="final_norm",
    )(h, peer_t, g.reshape(1, d))


def nsa_pallas(q, kv, gates, kc, vcT):
    s = q.shape[0]
    tq, k = Q_BLOCK, KV_WIDTH
    nq, nt = s // tq, s // KV_TILE
    n_sel = min(N_SEL, s // SEL_BLOCK)
    tile_k = lambda v: v.reshape(nt, KV_TILE, N_KV, HEAD_DIM).transpose(2, 0, 1, 3)
    tile_vT = lambda v: v.reshape(nt, KV_TILE, N_KV, HEAD_DIM).transpose(2, 0, 3, 1)
    qT = q.reshape(nq, tq, N_KV, HPG, HEAD_DIM).transpose(2, 0, 4, 3, 1).reshape(N_KV, nq, HEAD_DIM, HPG * tq)
    gateT = gates[:, :3 * N_HEADS].reshape(nq, tq, N_KV, HPG, 3).transpose(2, 0, 4, 3, 1).reshape(N_KV, nq, 3, HPG * tq)
    oT = _nsa_pallas(qT, gateT, kc, vcT, tile_k(kv[:, 2 * k:3 * k]), tile_vT(kv[:, 3 * k:4 * k]),
                     tile_k(kv[:, 4 * k:5 * k]), tile_vT(kv[:, 5 * k:6 * k]), tq=tq, n_sel=n_sel)
    return oT.reshape(N_KV, nq, HEAD_DIM, HPG, tq).transpose(1, 4, 0, 3, 2).reshape(s, ATTN_WIDTH)


def _final_norm_kernel(h_ref, g_ref, o_ref):
    xf = h_ref[...]
    y = xf * lax.rsqrt(jnp.mean(xf * xf, axis=-1, keepdims=True) + NORM_EPS)
    o_ref[...] = y * g_ref[...]


def _final_norm_pallas(h, g):
    b, s, d = h.shape
    tb = 512
    out = pl.pallas_call(
        _final_norm_kernel,
        grid=(b * s // tb,),
        in_specs=[pl.BlockSpec((tb, d), lambda i: (i, 0)), pl.BlockSpec((1, d), lambda i: (0, 0))],
        out_specs=pl.BlockSpec((tb, d), lambda i: (i, 0)),
        out_shape=jax.ShapeDtypeStruct((b * s, d), h.dtype),
        name="final_norm",
    )(h.reshape(b * s, d), g.reshape(1, d))
    return out.reshape(b, s, d)


def kernel(x, attn_norm, w_in, cmp_k_pe, cmp_k_w1, cmp_k_w2, cmp_v_pe, cmp_v_w1, cmp_v_w2, ssm_lam_re, ssm_lam_im, ssm_log_dt, ssm_b_re, ssm_b_im, ssm_c_re, ssm_c_im, ssm_d, ssm_w_glu, w_out, ffn_norm, peer_w_q, peer_subkeys_1, peer_subkeys_2, peer_u, peer_v, final_norm):
    return layer_forward(x, attn_norm, w_in, cmp_k_pe, cmp_k_w1, cmp_k_w2, cmp_v_pe, cmp_v_w1, cmp_v_w2, ssm_lam_re,
                         ssm_lam_im, ssm_log_dt, ssm_b_re, ssm_b_im, ssm_c_re, ssm_c_im, ssm_d, ssm_w_glu, w_out,
                         ffn_norm, peer_w_q, peer_subkeys_1, peer_subkeys_2, peer_u, peer_v, final_norm)


def layer_forward(x, attn_norm, w_in, cmp_k_pe, cmp_k_w1, cmp_k_w2, cmp_v_pe, cmp_v_w1, cmp_v_w2, ssm_lam_re,
                  ssm_lam_im, ssm_log_dt, ssm_b_re, ssm_b_im, ssm_c_re, ssm_c_im, ssm_d, ssm_w_glu, w_out,
                  ffn_norm, peer_w_q, peer_subkeys_1, peer_subkeys_2, peer_u, peer_v, final_norm):
    b, s, d = x.shape
    assert b == 1 and attn_norm.shape[0] == 1 and s % max(TOK_TILE, KV_TILE) == 0
    h = x[0]
    q, kv, u, gates = in_proj_pallas(h, attn_norm[0], w_in[0])
    kc, vcT = compress_pallas(kv[:, :KV_WIDTH], kv[:, KV_WIDTH:2 * KV_WIDTH], cmp_k_pe[0], cmp_k_w1[0], cmp_k_w2[0],
                              cmp_v_pe[0], cmp_v_w1[0], cmp_v_w2[0])
    a_out = nsa_pallas(q, kv, gates, kc, vcT)
    y = ssm_scan_pallas(u, ssm_lam_re[0], ssm_lam_im[0], ssm_log_dt[0], ssm_b_re[0], ssm_b_im[0],
                        ssm_c_re[0], ssm_c_im[0])
    h1 = mix_pallas(h, a_out, y, u, ssm_d[0], ssm_w_glu[0], w_out[0])
    peer_t = peer_ffn_pallas(h1, ffn_norm[0], peer_w_q[0], peer_subkeys_1[0], peer_subkeys_2[0], peer_u[0], peer_v[0])
    return final_pallas(h1, peer_t, final_norm)[None]
```

```python
import functools
import jax, jax.numpy as jnp
from jax import lax
from jax.experimental import pallas as pl
from jax.experimental.pallas import tpu as pltpu

N_HEADS = 8
HEAD_DIM = 64
N_KV = 2
HPG = N_HEADS // N_KV
ATTN_WIDTH = N_HEADS * HEAD_DIM
SSM_GROUP = 16
SSM_GROUPS = 32
SSM_WIDTH = SSM_GROUP * SSM_GROUPS
SSM_STATE = 64
MIX_WIDTH = ATTN_WIDTH + SSM_WIDTH
KV_WIDTH = N_KV * HEAD_DIM
CMP_LEN = 32
CMP_STRIDE = 16
CMP_HIDDEN = 128
SEL_BLOCK = 64
N_SEL = 16
WINDOW = 512
Q_BLOCK = 128
ROPE_THETA = 500000.0
ROPE_DIM = HEAD_DIM // 4
PEER_HEADS = 8
PEER_NKEYS = 128
PEER_QDIM = 256
PEER_HALF = PEER_QDIM // 2
PEER_TOPK = 16
NORM_EPS = 1e-6
NEG_INF = -1e30
FORCE_SCORE = 1e9
BIG = 3e38
LOG2E = 1.4426950408889634
HI = lax.Precision.HIGHEST

VMEM_LIMIT = 56 * 1024 * 1024
TOK_TILE = 512
KV_TILE = 512
BLOCKS_PER_TILE = KV_TILE // SEL_BLOCK
AUG = 16
K_AUG = HEAD_DIM + AUG
MASK_BIAS = 1e30
PROJ_KV = 6 * KV_WIDTH
ROPE_COLS = ATTN_WIDTH + 2 * KV_WIDTH
GATE_PAD = 128
SSM_CHUNK = 16
PEER_EXP_TILE = 512
NO_RANK = 64.0


def _params(*sem, flags=None):
    return pltpu.CompilerParams(dimension_semantics=sem, vmem_limit_bytes=VMEM_LIMIT, flags=flags)


def _rope_tables(pos, width):
    half = ROPE_DIM // 2
    inv = ROPE_THETA ** (-jnp.arange(half, dtype=jnp.float32) * 2.0 / ROPE_DIM)
    ang = pos.astype(jnp.float32)[:, None] * inv[None, :]
    n = pos.shape[0]
    cos = jnp.concatenate([jnp.cos(ang), jnp.cos(ang), jnp.ones((n, HEAD_DIM - ROPE_DIM), jnp.float32)], axis=1)
    sin = jnp.concatenate([jnp.sin(ang), jnp.sin(ang), jnp.zeros((n, HEAD_DIM - ROPE_DIM), jnp.float32)], axis=1)
    reps = width // HEAD_DIM
    return jnp.tile(cos, (1, reps)), jnp.tile(sin, (1, reps))


def _rope_partner(w):
    shp = w.shape
    wh = w.reshape(shp[:-1] + (shp[-1] // HEAD_DIM, HEAD_DIM))
    half = ROPE_DIM // 2
    out = jnp.concatenate([-wh[..., half:ROPE_DIM], wh[..., :half], jnp.zeros_like(wh[..., ROPE_DIM:])], axis=-1)
    return out.reshape(shp)


def _in_proj_kernel(x_ref, g_ref, w_ref, wp_ref, cos_ref, sin_ref, q_ref, kv_ref, u_ref, gate_ref):
    f32, bf16 = jnp.float32, jnp.bfloat16
    x = x_ref[...]
    z = (x * lax.rsqrt(jnp.mean(x * x, axis=-1, keepdims=True) + NORM_EPS) * g_ref[...]).astype(bf16)
    proj = jnp.dot(z, w_ref[...], preferred_element_type=f32)
    part = jnp.dot(z, wp_ref[...], preferred_element_type=f32)
    cos2, sin2 = cos_ref[...], sin_ref[...]

    def rot(v, p):
        reps = v.shape[1] // 128
        return v * jnp.concatenate([cos2] * reps, axis=1) + p * jnp.concatenate([sin2] * reps, axis=1)

    a, k = ATTN_WIDTH, KV_WIDTH
    q_ref[...] = (rot(proj[:, :a], part[:, :a]) * (HEAD_DIM ** -0.5 * LOG2E)).astype(bf16)
    kv = proj[:, a:a + PROJ_KV]
    ks = rot(kv[:, 2 * k:3 * k], part[:, a:a + k])
    kw = rot(kv[:, 4 * k:5 * k], part[:, a + k:a + 2 * k])
    kv_ref[...] = jnp.concatenate([kv[:, :2 * k], ks, kv[:, 3 * k:4 * k], kw, kv[:, 5 * k:]], axis=1).astype(bf16)
    u_ref[...] = proj[:, a + PROJ_KV:a + PROJ_KV + SSM_WIDTH]
    gate_ref[...] = proj[:, a + PROJ_KV + SSM_WIDTH:]


def in_proj_pallas(x, g, w_in):
    f32, bf16 = jnp.float32, jnp.bfloat16
    s, d = x.shape
    a, k = ATTN_WIDTH, KV_WIDTH
    c_kv, c_gate, c_u = a, a + PROJ_KV, a + PROJ_KV + 3 * N_HEADS
    w = jnp.concatenate([w_in[:, :c_gate], w_in[:, c_u:], w_in[:, c_gate:c_u],
                         jnp.zeros((d, GATE_PAD - 3 * N_HEADS), f32)], axis=1)
    wp = _rope_partner(jnp.concatenate([w_in[:, :a], w_in[:, c_kv + 2 * k:c_kv + 3 * k],
                                        w_in[:, c_kv + 4 * k:c_kv + 5 * k]], axis=1))
    cos2, sin2 = _rope_tables(jnp.arange(s), 128)
    tt = min(TOK_TILE, s)
    tok = lambda n: pl.BlockSpec((tt, n), lambda t: (t, 0))
    full = lambda r, n: pl.BlockSpec((r, n), lambda t: (0, 0))
    return pl.pallas_call(
        _in_proj_kernel,
        grid=(s // tt,),
        in_specs=[tok(d), full(1, d), full(d, w.shape[1]), full(d, ROPE_COLS), tok(128), tok(128)],
        out_specs=[tok(a), tok(PROJ_KV), tok(SSM_WIDTH), tok(GATE_PAD)],
        out_shape=[jax.ShapeDtypeStruct((s, a), bf16), jax.ShapeDtypeStruct((s, PROJ_KV), bf16),
                   jax.ShapeDtypeStruct((s, SSM_WIDTH), f32), jax.ShapeDtypeStruct((s, GATE_PAD), f32)],
        compiler_params=_params("arbitrary"),
        name="in_proj",
    )(x, g.reshape(1, d), w.astype(bf16), wp.astype(bf16), cos2, sin2)


def _compress_kernel(rk_ref, rv_ref, pek_ref, pev_ref, w1k_ref, w1v_ref, w2k_ref, w2vT_ref, cos_ref, sin_ref,
                     swap_ref, kc_ref, vcT_ref):
    f32, bf16 = jnp.float32, jnp.bfloat16
    ncp = rk_ref.shape[1]
    half = CMP_STRIDE * HEAD_DIM
    rk = rk_ref[0]
    first = jnp.dot(rk, w1k_ref[:half, :], preferred_element_type=f32)
    second = jnp.dot(rk, w1k_ref[half:, :], preferred_element_type=f32)
    pe_term = jnp.dot(pek_ref[...], w1k_ref[...], preferred_element_type=f32)
    hid = jax.nn.gelu(first + pltpu.roll(second, ncp - 1, axis=0) + pe_term)
    kc = jnp.dot(hid.astype(bf16), w2k_ref[...], preferred_element_type=f32)
    partner = jnp.dot(kc, swap_ref[...], precision=HI, preferred_element_type=f32)
    kc_ref[0] = (kc * cos_ref[...] + partner * sin_ref[...]).astype(bf16)
    rv = rv_ref[0]
    nt = (((1,), (1,)), ((), ()))
    first_t = lax.dot_general(w1v_ref[:, :half], rv, nt, preferred_element_type=f32)
    second_t = lax.dot_general(w1v_ref[:, half:], rv, nt, preferred_element_type=f32)
    pe_t = jnp.sum(w1v_ref[...].astype(f32) * pev_ref[...].astype(f32), axis=1, keepdims=True)
    hid_t = jax.nn.gelu(first_t + pltpu.roll(second_t, ncp - 1, axis=1) + pe_t)
    vcT_ref[0] = jnp.dot(w2vT_ref[...], hid_t.astype(bf16), preferred_element_type=f32).astype(bf16)


def compress_pallas(kc_raw, vc_raw, pe_k, w1k, w2k, pe_v, w1v, w2v):
    f32, bf16 = jnp.float32, jnp.bfloat16
    s = kc_raw.shape[0]
    ncp = s // CMP_STRIDE
    wide = CMP_STRIDE * HEAD_DIM
    regroup = lambda v: v.reshape(ncp, CMP_STRIDE, N_KV, HEAD_DIM).transpose(2, 0, 1, 3).reshape(N_KV, ncp, wide)
    cos, sin = _rope_tables(jnp.arange(ncp) * CMP_STRIDE + CMP_LEN - 1, HEAD_DIM)
    swap = _rope_partner(jnp.eye(HEAD_DIM, dtype=f32))
    per_g = lambda g: (g, 0, 0)
    full = lambda r, n: pl.BlockSpec((r, n), lambda g: (0, 0))
    return pl.pallas_call(
        _compress_kernel,
        grid=(N_KV,),
        in_specs=[pl.BlockSpec((1, ncp, wide), per_g), pl.BlockSpec((1, ncp, wide), per_g),
                  full(1, 2 * wide), full(1, 2 * wide), full(2 * wide, CMP_HIDDEN), full(CMP_HIDDEN, 2 * wide),
                  full(CMP_HIDDEN, HEAD_DIM), full(HEAD_DIM, CMP_HIDDEN), full(ncp, HEAD_DIM), full(ncp, HEAD_DIM),
                  full(HEAD_DIM, HEAD_DIM)],
        out_specs=[pl.BlockSpec((1, ncp, HEAD_DIM), per_g), pl.BlockSpec((1, HEAD_DIM, ncp), per_g)],
        out_shape=[jax.ShapeDtypeStruct((N_KV, ncp, HEAD_DIM), bf16), jax.ShapeDtypeStruct((N_KV, HEAD_DIM, ncp), bf16)],
        compiler_params=_params("arbitrary"),
        name="compress",
    )(regroup(kc_raw), regroup(vc_raw), pe_k.reshape(1, 2 * wide).astype(bf16), pe_v.reshape(1, 2 * wide).astype(bf16),
      w1k.astype(bf16), w1v.T.astype(bf16), w2k.astype(bf16), w2v.T.astype(bf16), cos, sin, swap)


def _nsa_kernel(qT_ref, gate_ref, kc_ref, vcT_ref, ks_ref, vsT_ref, kw_ref, vwT_ref, o_ref,
                ps_ref, pen_ref, s_scr, p_scr, cm_scr, m_scr, acc_scr, *, tq, n_sel):
    f32, bf16 = jnp.float32, jnp.bfloat16
    i = pl.program_id(1)
    qT = qT_ref[0, 0]
    ncol = qT.shape[1]
    ncp = kc_ref.shape[1]
    nsb = pen_ref.shape[0]
    t0 = i * tq
    t_col = t0 + (lax.broadcasted_iota(jnp.int32, (1, ncol), 1) & (tq - 1))
    t_tok = t0 + lax.broadcasted_iota(jnp.int32, (1, tq), 1)

    s = jnp.dot(kc_ref[0], qT, preferred_element_type=f32)
    n_idx = lax.broadcasted_iota(jnp.int32, (ncp, ncol), 0)
    cmask = (n_idx * CMP_STRIDE + (CMP_LEN - 1)) <= t_col
    s = jnp.where(cmask, s, NEG_INF)
    m = jnp.max(s, axis=0, keepdims=True)
    e = jnp.where(cmask, jnp.exp2(s - m), 0.0)
    l = jnp.sum(e, axis=0, keepdims=True)
    p = e / jnp.maximum(l, 1e-30)
    o_c = jnp.dot(vcT_ref[0], p.astype(bf16), preferred_element_type=f32)

    psum = p[:, 0:tq]
    for h in range(1, HPG):
        psum = psum + p[:, h * tq:(h + 1) * tq]
    ps_ref[...] = psum
    parts = [ps_ref[pl.ds(k, nsb, stride=4), :] for k in range(4)]
    m_idx = lax.broadcasted_iota(jnp.int32, (nsb, tq), 0)
    prev = jnp.where(m_idx == 0, 0.0, pltpu.roll(parts[3], 1, axis=0))
    imp = parts[0] + parts[1] + parts[2] + parts[3] + prev

    cur = t_tok // SEL_BLOCK
    forced = (m_idx == 0) | (m_idx == cur) | (m_idx == cur - 1)
    valid = m_idx <= cur
    val0 = jnp.where(forced, FORCE_SCORE, jnp.where(valid, imp, NEG_INF))
    m_f = m_idx.astype(f32)

    def pick(_, carry):
        val, sel = carry
        mx = jnp.max(val, axis=0, keepdims=True)
        first = jnp.min(jnp.where(val == mx, m_f, 1e9), axis=0, keepdims=True)
        hit = m_f == first
        return jnp.where(hit, -BIG, val), jnp.where(hit, 1.0, sel)

    _, sel = lax.fori_loop(0, n_sel, pick, (val0, jnp.zeros((nsb, tq), f32)))
    pen_ref[...] = jnp.where(valid & (sel > 0.5), 0.0, -MASK_BIAS)

    def sweep(k, vT, q_op, mask, carry):
        m_i, acc = carry
        sj = jnp.dot(k, q_op, preferred_element_type=f32)
        if mask is not None:
            sj = jnp.where(mask, sj, NEG_INF)
        m_new = jnp.maximum(m_i, jnp.max(sj, axis=0, keepdims=True))
        alpha = jnp.exp2(m_i - m_new)
        pj = jnp.exp2(sj - m_new).astype(bf16)
        return m_new, alpha * acc + jnp.dot(vT, pj, preferred_element_type=f32)

    def q_with_penalty(j):
        pen = pen_ref[pl.ds(pl.multiple_of(j * BLOCKS_PER_TILE, BLOCKS_PER_TILE), BLOCKS_PER_TILE), :]
        extra = jnp.concatenate([jnp.concatenate([pen] * HPG, axis=1),
                                 jnp.zeros((AUG - BLOCKS_PER_TILE, ncol), f32)], axis=0)
        return jnp.concatenate([qT, extra.astype(bf16)], axis=0)

    def kpos_of(j):
        return j * KV_TILE + lax.broadcasted_iota(jnp.int32, (KV_TILE, ncol), 0)

    init = (jnp.full((1, ncol), NEG_INF, f32), jnp.zeros((K_AUG, ncol), f32))
    jd = t0 // KV_TILE

    def issue_scores(j, slot):
        sj = jnp.dot(ks_ref[0, j], q_with_penalty(j), preferred_element_type=f32)
        s_scr[slot] = sj
        cm_scr[slot] = jnp.max(sj, axis=0, keepdims=True)

    def stage(j, slot):
        pv = jnp.dot(vsT_ref[0, jnp.maximum(j - 1, 0)], p_scr[1 - slot], preferred_element_type=f32)
        issue_scores(j + 1, 1 - slot)
        m_i = m_scr[...]
        m_new = jnp.maximum(m_i, cm_scr[slot])
        p_scr[slot] = jnp.exp2(s_scr[slot] - m_new).astype(bf16)
        acc_scr[...] = jnp.exp2(m_i - m_new) * (acc_scr[...] + pv)
        m_scr[...] = m_new

    m_scr[...] = init[0]
    acc_scr[...] = init[1]
    p_scr[1] = jnp.zeros((KV_TILE, ncol), bf16)
    j0 = jd % 2

    @pl.when(j0 == 1)
    def _():
        m_new, acc = sweep(ks_ref[0, 0], vsT_ref[0, 0], q_with_penalty(0), None, init)
        m_scr[...] = m_new
        acc_scr[...] = acc

    issue_scores(j0, 0)

    def stage_pair(jj, _):
        j = j0 + 2 * jj
        stage(j, 0)
        stage(j + 1, 1)
        return 0

    lax.fori_loop(0, (jd - j0) // 2, stage_pair, 0)
    pv = jnp.dot(vsT_ref[0, jnp.maximum(jd - 1, 0)], p_scr[1], preferred_element_type=f32)
    sd = jnp.where(kpos_of(jd) <= t_col, s_scr[0], NEG_INF)
    m_i = m_scr[...]
    m_new = jnp.maximum(m_i, jnp.max(sd, axis=0, keepdims=True))
    acc_s = (jnp.exp2(m_i - m_new) * (acc_scr[...] + pv)
             + jnp.dot(vsT_ref[0, jd], jnp.exp2(sd - m_new).astype(bf16), preferred_element_type=f32))

    def win_step(j, c):
        kpos = kpos_of(j)
        return sweep(kw_ref[0, j], vwT_ref[0, j], qT, (kpos <= t_col) & (t_col - kpos < WINDOW), c)

    jw0 = jnp.maximum(t0 - (WINDOW - 1), 0) // KV_TILE
    _, acc_w = lax.fori_loop(jw0, jd + 1, win_step, init)

    norm = lambda acc: acc[:HEAD_DIM] / jnp.maximum(acc[HEAD_DIM:HEAD_DIM + 1], 1e-30)
    g = jax.nn.sigmoid(gate_ref[0, 0])
    o_ref[0, 0] = g[0:1] * o_c + g[1:2] * norm(acc_s) + g[2:3] * norm(acc_w)


def nsa_pallas(q, kv, gates, kc, vcT):
    bf16 = jnp.bfloat16
    s = q.shape[0]
    tq, k = Q_BLOCK, KV_WIDTH
    nq, nt = s // tq, s // KV_TILE
    ncol = HPG * tq
    ncp = kc.shape[1]
    nsb = s // SEL_BLOCK
    n_sel = min(N_SEL, nsb)
    heads = lambda v: v.reshape(s, N_KV, HEAD_DIM)
    blk = (jnp.arange(s) // SEL_BLOCK) % BLOCKS_PER_TILE
    one_hot = (blk[:, None] == jnp.arange(AUG)[None, :]).astype(bf16)
    ones_col = (jnp.arange(AUG) == 0).astype(bf16)
    widen = lambda v, extra: jnp.concatenate([heads(v), jnp.broadcast_to(extra, (s, N_KV, AUG))], axis=2)
    ks = widen(kv[:, 2 * k:3 * k], one_hot[:, None, :]).reshape(nt, KV_TILE, N_KV, K_AUG).transpose(2, 0, 1, 3)
    kw = heads(kv[:, 4 * k:5 * k]).reshape(nt, KV_TILE, N_KV, HEAD_DIM).transpose(2, 0, 1, 3)
    tile_vT = lambda v: widen(v, ones_col[None, None, :]).reshape(nt, KV_TILE, N_KV, K_AUG).transpose(2, 0, 3, 1)
    qT = q.reshape(nq, tq, N_KV, HPG, HEAD_DIM).transpose(2, 0, 4, 3, 1).reshape(N_KV, nq, HEAD_DIM, ncol)
    gateT = gates[:, :3 * N_HEADS].reshape(nq, tq, N_KV, HPG, 3).transpose(2, 0, 4, 3, 1).reshape(N_KV, nq, 3, ncol)
    per_q = lambda g, i: (g, i, 0, 0)
    per_g3 = lambda g, i: (g, 0, 0)
    per_g4 = lambda g, i: (g, 0, 0, 0)
    oT = pl.pallas_call(
        functools.partial(_nsa_kernel, tq=tq, n_sel=n_sel),
        grid=(N_KV, nq),
        in_specs=[pl.BlockSpec((1, 1, HEAD_DIM, ncol), per_q),
                  pl.BlockSpec((1, 1, 3, ncol), per_q),
                  pl.BlockSpec((1, ncp, HEAD_DIM), per_g3),
                  pl.BlockSpec((1, HEAD_DIM, ncp), per_g3),
                  pl.BlockSpec((1, nt, KV_TILE, K_AUG), per_g4),
                  pl.BlockSpec((1, nt, K_AUG, KV_TILE), per_g4),
                  pl.BlockSpec((1, nt, KV_TILE, HEAD_DIM), per_g4),
                  pl.BlockSpec((1, nt, K_AUG, KV_TILE), per_g4)],
        out_specs=pl.BlockSpec((1, 1, HEAD_DIM, ncol), per_q),
        out_shape=jax.ShapeDtypeStruct((N_KV, nq, HEAD_DIM, ncol), jnp.float32),
        scratch_shapes=[pltpu.VMEM((ncp, tq), jnp.float32), pltpu.VMEM((nsb, tq), jnp.float32),
                        pltpu.VMEM((2, KV_TILE, ncol), jnp.float32), pltpu.VMEM((2, KV_TILE, ncol), bf16),
                        pltpu.VMEM((2, 1, ncol), jnp.float32), pltpu.VMEM((1, ncol), jnp.float32),
                        pltpu.VMEM((K_AUG, ncol), jnp.float32)],
        compiler_params=_params("arbitrary", "arbitrary"),
        name="nsa_attention",
    )(qT, gateT, kc, vcT, ks, tile_vT(kv[:, 3 * k:4 * k]), kw, tile_vT(kv[:, 5 * k:6 * k]))
    return oT.reshape(N_KV, nq, HEAD_DIM, HPG, tq).transpose(1, 4, 0, 3, 2).reshape(s, ATTN_WIDTH)


def _cplx_pow(lr_dt, li_dt, n):
    mag = jnp.exp(n * lr_dt)
    return mag * jnp.cos(n * li_dt), mag * jnp.sin(n * li_dt)


def _ssm_kernel(u_ref, ldt_ref, lr_row_ref, li_row_ref, lr_col_ref, li_col_ref,
                bre_t_ref, bim_t_ref, cre_t_ref, cim_t_ref, y_ref):
    f32, bf16 = jnp.float32, jnp.bfloat16
    L, C, P = SSM_CHUNK, SSM_GROUP, SSM_STATE
    nc = u_ref.shape[1]
    dt = jnp.exp(ldt_ref[0])

    lr_r, li_r = lr_row_ref[0], li_row_ref[0]
    ar, ai = _cplx_pow(lr_r * dt, li_r * dt, 1.0)
    den = lr_r * lr_r + li_r * li_r
    fr_r = ((ar - 1.0) * lr_r + ai * li_r) / den
    fi_r = (ai * lr_r - (ar - 1.0) * li_r) / den
    bbt_re = fr_r * bre_t_ref[0] - fi_r * bim_t_ref[0]
    bbt_im = fr_r * bim_t_ref[0] + fi_r * bre_t_ref[0]
    lr_c, li_c = lr_col_ref[0], li_col_ref[0]
    lane = lax.broadcasted_iota(jnp.int32, (1, L * C), 1)
    step = (lane // C).astype(f32)
    tile = (lax.broadcasted_iota(jnp.int32, (C, L * C), 0) == (lane % C)).astype(f32)
    cmt_re = jnp.dot(cre_t_ref[0], tile, precision=HI, preferred_element_type=f32)
    cmt_im = jnp.dot(cim_t_ref[0], tile, precision=HI, preferred_element_type=f32)

    def c_times_pow(n):
        pr, pi = _cplx_pow(lr_c * dt, li_c * dt, n)
        return pr * cmt_re - pi * cmt_im, pr * cmt_im + pi * cmt_re

    w_re, w_im = c_times_pow(step)
    taps = (jnp.dot(bbt_re, w_re, precision=HI, preferred_element_type=f32)
            - jnp.dot(bbt_im, w_im, precision=HI, preferred_element_type=f32))
    slabs = []
    for r in range(L):
        sl = taps if r == 0 else jnp.where(lane >= r * C, pltpu.roll(taps, r * C, axis=1), 0.0)
        slabs.append(sl)
    t_mat = jnp.concatenate(slabs, axis=0).astype(bf16)

    row_step = (lax.broadcasted_iota(jnp.int32, (L * C, 1), 0) // C).astype(f32)
    pr, pi = _cplx_pow(lr_r * dt, li_r * dt, (L - 1.0) - row_step)
    bt_re = jnp.concatenate([bbt_re] * L, axis=0)
    bt_im = jnp.concatenate([bbt_im] * L, axis=0)
    m_re = (pr * bt_re - pi * bt_im).astype(bf16)
    m_im = (pr * bt_im + pi * bt_re).astype(bf16)
    n_re, n_im = c_times_pow(step + 1.0)

    u = u_ref[0]
    e_re = jnp.dot(u, m_re, preferred_element_type=f32)
    e_im = jnp.dot(u, m_im, preferred_element_type=f32)
    k_idx = lax.broadcasted_iota(jnp.int32, (nc, P), 0)
    qr, qi = _cplx_pow(lr_r * dt, li_r * dt, float(L))
    d = 1
    while d < nc:
        sr = jnp.where(k_idx >= d, pltpu.roll(e_re, d, axis=0), 0.0)
        si = jnp.where(k_idx >= d, pltpu.roll(e_im, d, axis=0), 0.0)
        e_re, e_im = e_re + qr * sr - qi * si, e_im + qr * si + qi * sr
        qr, qi = qr * qr - qi * qi, 2.0 * qr * qi
        d *= 2
    x_re = jnp.where(k_idx >= 1, pltpu.roll(e_re, 1, axis=0), 0.0)
    x_im = jnp.where(k_idx >= 1, pltpu.roll(e_im, 1, axis=0), 0.0)
    y = jnp.dot(u, t_mat, preferred_element_type=f32)
    y = y + jnp.dot(x_re.astype(bf16), n_re.astype(bf16), preferred_element_type=f32)
    y = y - jnp.dot(x_im.astype(bf16), n_im.astype(bf16), preferred_element_type=f32)
    y_ref[0] = y


def ssm_scan_pallas(u, lam_re, lam_im, log_dt, b_re, b_im, c_re, c_im):
    f32, bf16 = jnp.float32, jnp.bfloat16
    s = u.shape[0]
    G, C, P, L = SSM_GROUPS, SSM_GROUP, SSM_STATE, SSM_CHUNK
    nc = s // L
    ug = u.reshape(nc, L, G, C).transpose(2, 0, 1, 3).reshape(G, nc, L * C).astype(bf16)
    per_g = lambda g: (g, 0, 0)
    spec = lambda a, b: pl.BlockSpec((1, a, b), per_g)
    y = pl.pallas_call(
        _ssm_kernel,
        grid=(G,),
        in_specs=[spec(nc, L * C), spec(1, 1), spec(1, P), spec(1, P), spec(P, 1), spec(P, 1),
                  spec(C, P), spec(C, P), spec(P, C), spec(P, C)],
        out_specs=spec(nc, L * C),
        out_shape=jax.ShapeDtypeStruct((G, nc, L * C), f32),
        compiler_params=_params("arbitrary"),
        name="ssm_scan",
    )(ug, log_dt.reshape(G, 1, 1), lam_re.reshape(G, 1, P), lam_im.reshape(G, 1, P),
      lam_re.reshape(G, P, 1), lam_im.reshape(G, P, 1),
      b_re.transpose(0, 2, 1), b_im.transpose(0, 2, 1), c_re.transpose(0, 2, 1), c_im.transpose(0, 2, 1))
    return y.reshape(G, nc, L, C).transpose(1, 2, 0, 3).reshape(s, G * C)


def _mix_kernel(h_ref, a_ref, y_ref, u_ref, d_ref, wglu_ref, wout_ref, o_ref):
    f32, bf16 = jnp.float32, jnp.bfloat16
    y = jax.nn.gelu(y_ref[...] + d_ref[...] * u_ref[...]).astype(bf16)
    ab = jnp.dot(y, wglu_ref[...], preferred_element_type=f32)
    s_out = ab[:, :SSM_WIDTH] * jax.nn.sigmoid(ab[:, SSM_WIDTH:])
    mix = jnp.concatenate([a_ref[...].astype(bf16), s_out.astype(bf16)], axis=1)
    o_ref[...] = h_ref[...] + jnp.dot(mix, wout_ref[...], preferred_element_type=f32)


def mix_pallas(h, a_out, y, u, d_skip, w_glu, w_out):
    bf16 = jnp.bfloat16
    s, d = h.shape
    tt = min(TOK_TILE, s)
    tok = lambda n: pl.BlockSpec((tt, n), lambda t: (t, 0))
    full = lambda r, n: pl.BlockSpec((r, n), lambda t: (0, 0))
    return pl.pallas_call(
        _mix_kernel,
        grid=(s // tt,),
        in_specs=[tok(d), tok(ATTN_WIDTH), tok(SSM_WIDTH), tok(SSM_WIDTH), full(1, SSM_WIDTH),
                  full(SSM_WIDTH, 2 * SSM_WIDTH), full(MIX_WIDTH, d)],
        out_specs=tok(d),
        out_shape=jax.ShapeDtypeStruct((s, d), jnp.float32),
        compiler_params=_params("arbitrary"),
        name="mix_out",
    )(h, a_out, y, u, d_skip.reshape(1, SSM_WIDTH), w_glu.astype(bf16), w_out.astype(bf16))


def _ranked_extract(val, k, v_scr, off):
    r_f = lax.broadcasted_iota(jnp.int32, val.shape, 0).astype(jnp.float32)

    def body(a, carry):
        v, rank = carry
        mx = jnp.max(v, axis=0, keepdims=True)
        first = jnp.min(jnp.where(v == mx, r_f, 1e9), axis=0, keepdims=True)
        hit = r_f == first
        v_scr[pl.ds(off + a, 1), :] = mx
        return jnp.where(hit, -BIG, v), jnp.where(hit, a.astype(jnp.float32), rank)

    _, rank = lax.fori_loop(0, k, body, (val, jnp.full(val.shape, NO_RANK, jnp.float32)))
    return rank


def _peer_score_kernel(h_ref, g_ref, wqT_ref, sub1_ref, sub2_ref, xnT_ref, r1_ref, e1_ref, rr_ref, e2_ref,
                       q_scr, v_scr):
    f32, bf16 = jnp.float32, jnp.bfloat16
    k = PEER_TOPK
    x = h_ref[...]
    xn = x * lax.rsqrt(jnp.mean(x * x, axis=-1, keepdims=True) + NORM_EPS) * g_ref[...]
    xnT_ref[...] = xn.T.astype(bf16)
    q_scr[...] = lax.dot_general(wqT_ref[...], xn.astype(bf16), (((1,), (1,)), ((), ())),
                                 preferred_element_type=f32)

    def head(h, _):
        base = pl.multiple_of(h * PEER_QDIM, PEER_QDIM)
        s1 = jnp.dot(sub1_ref[...], q_scr[pl.ds(base, PEER_HALF), :].astype(bf16), preferred_element_type=f32)
        s2 = jnp.dot(sub2_ref[...], q_scr[pl.ds(base + PEER_HALF, PEER_HALF), :].astype(bf16), preferred_element_type=f32)
        rank1 = _ranked_extract(s1, k, v_scr, 0)
        rank2 = _ranked_extract(s2, k, v_scr, k)
        v1 = v_scr[0:k, :]
        v2 = v_scr[k:2 * k, :]
        cand = jnp.concatenate([v1[0:1] + v2] + [v1[a:a + 1] + v2[0:8] for a in range(1, 8)] + [v1[8:k] + v2[0:1]],
                               axis=0)
        picked = (_ranked_extract(cand, k, v_scr, 2 * k) < NO_RANK).astype(f32)
        m1, m2 = v1[0:1], v2[0:1]
        z = jnp.sum(picked * jnp.exp(cand - (m1 + m2)), axis=0, keepdims=True)
        low = picked[0:8]
        for a in range(1, 8):
            low = low + picked[8 + 8 * a:16 + 8 * a]
        row0 = lax.broadcasted_iota(jnp.int32, low.shape, 0) == 0
        low = low + jnp.where(row0, jnp.sum(picked[72:80], axis=0, keepdims=True), 0.0)
        counts = jnp.concatenate([low, picked[8:16]], axis=0)
        rr = jnp.zeros_like(rank2)
        for b in range(k):
            rr = jnp.where(rank2 == float(b), counts[b:b + 1], rr)
        r1_ref[h] = rank1
        e1_ref[h] = jnp.exp(s1 - m1)
        rr_ref[h] = rr.astype(bf16)
        e2_ref[h] = (jnp.exp(s2 - m2) / z).astype(bf16)
        return 0

    lax.fori_loop(0, PEER_HEADS, head, 0)


def _peer_expert_kernel(xnT_ref, r1_ref, e1_ref, rr_ref, e2_ref, u_ref, vT_ref, o_ref, a_scr, *, n_blocks):
    f32, bf16 = jnp.float32, jnp.bfloat16
    e = pl.program_id(1)
    half = u_ref.shape[0] // 2
    n_i = half // PEER_NKEYS

    @pl.when(e == 0)
    def _():
        o_ref[...] = jnp.zeros_like(o_ref)
        a_scr[...] = jnp.zeros_like(a_scr)

    i0 = jnp.minimum(e, n_blocks - 1) * (2 * n_i)

    def spread(row):
        tile = jnp.broadcast_to(row, (16, row.shape[1])).astype(bf16)
        return jnp.concatenate([tile] * (PEER_NKEYS // 16), axis=0)

    def gate(i):
        w = None
        for h in range(PEER_HEADS):
            r1_row = spread(r1_ref[h, pl.ds(i, 1), :])
            e1_row = spread(e1_ref[h, pl.ds(i, 1), :])
            term = jnp.where(r1_row < rr_ref[h], e2_ref[h], jnp.zeros((), bf16)) * e1_row
            w = term if w is None else w + term
        return w

    act = jax.nn.gelu(jnp.dot(u_ref[...], xnT_ref[...], preferred_element_type=f32).astype(bf16))
    rows = lambda k: slice(k * PEER_NKEYS, (k + 1) * PEER_NKEYS)
    for ii in range(n_i):
        a_scr[half + ii * PEER_NKEYS:half + (ii + 1) * PEER_NKEYS, :] = act[rows(ii)] * gate(i0 + ii)
    o_ref[...] += jnp.dot(vT_ref[...], a_scr[...], preferred_element_type=f32)
    for ii in range(n_i):
        a_scr[rows(ii), :] = act[rows(n_i + ii)] * gate(i0 + n_i + ii)


def peer_ffn_pallas(h1, g, w_q, sub1, sub2, u_tab, v_tab):
    f32, bf16 = jnp.float32, jnp.bfloat16
    s, d = h1.shape
    n_exp = u_tab.shape[0]
    tt = min(TOK_TILE, s)
    nq = PEER_HEADS * PEER_QDIM
    keys = lambda dt: jax.ShapeDtypeStruct((PEER_HEADS, PEER_NKEYS, s), dt)
    key_spec1 = pl.BlockSpec((PEER_HEADS, PEER_NKEYS, tt), lambda t: (0, 0, t))
    xnT, r1, e1, rr, e2 = pl.pallas_call(
        _peer_score_kernel,
        grid=(s // tt,),
        in_specs=[pl.BlockSpec((tt, d), lambda t: (t, 0)),
                  pl.BlockSpec((1, d), lambda t: (0, 0)),
                  pl.BlockSpec((nq, d), lambda t: (0, 0)),
                  pl.BlockSpec((PEER_NKEYS, PEER_HALF), lambda t: (0, 0)),
                  pl.BlockSpec((PEER_NKEYS, PEER_HALF), lambda t: (0, 0))],
        out_specs=[pl.BlockSpec((d, tt), lambda t: (0, t)), key_spec1, key_spec1, key_spec1, key_spec1],
        out_shape=[jax.ShapeDtypeStruct((d, s), bf16), keys(f32), keys(f32), keys(bf16), keys(bf16)],
        scratch_shapes=[pltpu.VMEM((nq, tt), f32), pltpu.VMEM((3 * PEER_TOPK, tt), f32)],
        compiler_params=_params("arbitrary"),
        name="peer_scores",
    )(h1, g.reshape(1, d), w_q.T.astype(bf16), sub1.astype(bf16), sub2.astype(bf16))
    te = PEER_EXP_TILE
    half = te // 2
    n_blocks = n_exp // te
    key_spec2 = pl.BlockSpec((PEER_HEADS, PEER_NKEYS, tt), lambda t, e: (0, 0, t))
    pad = jnp.zeros((d, half), bf16)
    v_t = jnp.concatenate([pad, v_tab.T.astype(bf16), pad], axis=1)
    return pl.pallas_call(
        functools.partial(_peer_expert_kernel, n_blocks=n_blocks),
        grid=(s // tt, n_blocks + 1),
        in_specs=[pl.BlockSpec((d, tt), lambda t, e: (0, t)), key_spec2, key_spec2, key_spec2, key_spec2,
                  pl.BlockSpec((te, d), lambda t, e: (jnp.minimum(e, n_blocks - 1), 0)),
                  pl.BlockSpec((d, te), lambda t, e: (0, e))],
        out_specs=pl.BlockSpec((d, tt), lambda t, e: (0, t)),
        out_shape=jax.ShapeDtypeStruct((d, s), f32),
        scratch_shapes=[pltpu.VMEM((te, tt), bf16)],
        compiler_params=_params("arbitrary", "arbitrary"),
        name="peer_experts",
    )(xnT, r1, e1, rr, e2, u_tab.astype(bf16), v_t)


def _final_kernel(h_ref, pT_ref, g_ref, o_ref):
    xf = h_ref[...] + pT_ref[...].T
    y = xf * lax.rsqrt(jnp.mean(xf * xf, axis=-1, keepdims=True) + NORM_EPS)
    o_ref[...] = y * g_ref[...]


def final_pallas(h, peer_t, g):
    s, d = h.shape
    tt = min(TOK_TILE, s)
    return pl.pallas_call(
        _final_kernel,
        grid=(s // tt,),
        in_specs=[pl.BlockSpec((tt, d), lambda t: (t, 0)), pl.BlockSpec((d, tt), lambda t: (0, t)),
                  pl.BlockSpec((1, d), lambda t: (0, 0))],
        out_specs=pl.BlockSpec((tt, d), lambda t: (t, 0)),
        out_shape=jax.ShapeDtypeStruct((s, d), jnp.float32),
        compiler_params=_params("arbitrary"),
        name="final_norm",
    )(h, peer_t, g.reshape(1, d))


def kernel(x, attn_norm, w_in, cmp_k_pe, cmp_k_w1, cmp_k_w2, cmp_v_pe, cmp_v_w1, cmp_v_w2, ssm_lam_re, ssm_lam_im, ssm_log_dt, ssm_b_re, ssm_b_im, ssm_c_re, ssm_c_im, ssm_d, ssm_w_glu, w_out, ffn_norm, peer_w_q, peer_subkeys_1, peer_subkeys_2, peer_u, peer_v, final_norm):
    b, s, d = x.shape
    assert b == 1 and attn_norm.shape[0] == 1 and s % max(TOK_TILE, KV_TILE) == 0
    h = x[0]
    q, kv, u, gates = in_proj_pallas(h, attn_norm[0], w_in[0])
    kc, vcT = compress_pallas(kv[:, :KV_WIDTH], kv[:, KV_WIDTH:2 * KV_WIDTH], cmp_k_pe[0], cmp_k_w1[0], cmp_k_w2[0],
                              cmp_v_pe[0], cmp_v_w1[0], cmp_v_w2[0])
    a_out = nsa_pallas(q, kv, gates, kc, vcT)
    y = ssm_scan_pallas(u, ssm_lam_re[0], ssm_lam_im[0], ssm_log_dt[0], ssm_b_re[0], ssm_b_im[0],
                        ssm_c_re[0], ssm_c_im[0])
    h1 = mix_pallas(h, a_out, y, u, ssm_d[0], ssm_w_glu[0], w_out[0])
    peer_t = peer_ffn_pallas(h1, ffn_norm[0], peer_w_q[0], peer_subkeys_1[0], peer_subkeys_2[0], peer_u[0], peer_v[0])
    return final_pallas(h1, peer_t, final_norm)[None]
```

```python
import functools
import jax, jax.numpy as jnp
from jax import lax
from jax.experimental import pallas as pl
from jax.experimental.pallas import tpu as pltpu

N_HEADS = 8
HEAD_DIM = 64
N_KV = 2
HPG = N_HEADS // N_KV
ATTN_WIDTH = N_HEADS * HEAD_DIM
SSM_GROUP = 16
SSM_GROUPS = 32
SSM_WIDTH = SSM_GROUP * SSM_GROUPS
SSM_STATE = 64
MIX_WIDTH = ATTN_WIDTH + SSM_WIDTH
KV_WIDTH = N_KV * HEAD_DIM
CMP_LEN = 32
CMP_STRIDE = 16
CMP_HIDDEN = 128
SEL_BLOCK = 64
N_SEL = 16
WINDOW = 512
Q_BLOCK = 128
ROPE_THETA = 500000.0
ROPE_DIM = HEAD_DIM // 4
PEER_HEADS = 8
PEER_NKEYS = 128
PEER_QDIM = 256
PEER_HALF = PEER_QDIM // 2
PEER_TOPK = 16
NORM_EPS = 1e-6
NEG_INF = -1e30
FORCE_SCORE = 1e9
BIG = 3e38
LOG2E = 1.4426950408889634
HI = lax.Precision.HIGHEST

VMEM_LIMIT = 56 * 1024 * 1024
TOK_TILE = 512
KV_TILE = 512
BLOCKS_PER_TILE = KV_TILE // SEL_BLOCK
AUG = 16
K_AUG = HEAD_DIM + AUG
MASK_BIAS = 1e30
WIN_TILES = (WINDOW + Q_BLOCK) // Q_BLOCK
PROJ_KV = 6 * KV_WIDTH
ROPE_COLS = ATTN_WIDTH + 2 * KV_WIDTH
GATE_PAD = 128
SSM_CHUNK = 16
PEER_EXP_TILE = 512
PEER_LANE_CHUNK = 256
NO_RANK = 64.0


def _params(*sem, flags=None):
    return pltpu.CompilerParams(dimension_semantics=sem, vmem_limit_bytes=VMEM_LIMIT, flags=flags)


def _rope_tables(pos, width):
    half = ROPE_DIM // 2
    inv = ROPE_THETA ** (-jnp.arange(half, dtype=jnp.float32) * 2.0 / ROPE_DIM)
    ang = pos.astype(jnp.float32)[:, None] * inv[None, :]
    n = pos.shape[0]
    cos = jnp.concatenate([jnp.cos(ang), jnp.cos(ang), jnp.ones((n, HEAD_DIM - ROPE_DIM), jnp.float32)], axis=1)
    sin = jnp.concatenate([jnp.sin(ang), jnp.sin(ang), jnp.zeros((n, HEAD_DIM - ROPE_DIM), jnp.float32)], axis=1)
    reps = width // HEAD_DIM
    return jnp.tile(cos, (1, reps)), jnp.tile(sin, (1, reps))


def _rope_partner(w):
    shp = w.shape
    wh = w.reshape(shp[:-1] + (shp[-1] // HEAD_DIM, HEAD_DIM))
    half = ROPE_DIM // 2
    out = jnp.concatenate([-wh[..., half:ROPE_DIM], wh[..., :half], jnp.zeros_like(wh[..., ROPE_DIM:])], axis=-1)
    return out.reshape(shp)


def _in_proj_kernel(x_ref, g_ref, w_ref, wp_ref, cos_ref, sin_ref, q_ref, kv_ref, u_ref, gate_ref):
    f32, bf16 = jnp.float32, jnp.bfloat16
    x = x_ref[...]
    z = (x * lax.rsqrt(jnp.mean(x * x, axis=-1, keepdims=True) + NORM_EPS) * g_ref[...]).astype(bf16)
    proj = jnp.dot(z, w_ref[...], preferred_element_type=f32)
    part = jnp.dot(z, wp_ref[...], preferred_element_type=f32)
    cos2, sin2 = cos_ref[...], sin_ref[...]

    def rot(v, p):
        reps = v.shape[1] // 128
        return v * jnp.concatenate([cos2] * reps, axis=1) + p * jnp.concatenate([sin2] * reps, axis=1)

    a, k = ATTN_WIDTH, KV_WIDTH
    q_ref[...] = (rot(proj[:, :a], part[:, :a]) * (HEAD_DIM ** -0.5 * LOG2E)).astype(bf16)
    kv = proj[:, a:a + PROJ_KV]
    ks = rot(kv[:, 2 * k:3 * k], part[:, a:a + k])
    kw = rot(kv[:, 4 * k:5 * k], part[:, a + k:a + 2 * k])
    kv_ref[...] = jnp.concatenate([kv[:, :2 * k], ks, kv[:, 3 * k:4 * k], kw, kv[:, 5 * k:]], axis=1).astype(bf16)
    u_ref[...] = proj[:, a + PROJ_KV:a + PROJ_KV + SSM_WIDTH]
    gate_ref[...] = proj[:, a + PROJ_KV + SSM_WIDTH:]


def in_proj_pallas(x, g, w_in):
    f32, bf16 = jnp.float32, jnp.bfloat16
    s, d = x.shape
    a, k = ATTN_WIDTH, KV_WIDTH
    c_kv, c_gate, c_u = a, a + PROJ_KV, a + PROJ_KV + 3 * N_HEADS
    w = jnp.concatenate([w_in[:, :c_gate], w_in[:, c_u:], w_in[:, c_gate:c_u],
                         jnp.zeros((d, GATE_PAD - 3 * N_HEADS), f32)], axis=1)
    wp = _rope_partner(jnp.concatenate([w_in[:, :a], w_in[:, c_kv + 2 * k:c_kv + 3 * k],
                                        w_in[:, c_kv + 4 * k:c_kv + 5 * k]], axis=1))
    cos2, sin2 = _rope_tables(jnp.arange(s), 128)
    tt = min(TOK_TILE, s)
    tok = lambda n: pl.BlockSpec((tt, n), lambda t: (t, 0))
    full = lambda r, n: pl.BlockSpec((r, n), lambda t: (0, 0))
    return pl.pallas_call(
        _in_proj_kernel,
        grid=(s // tt,),
        in_specs=[tok(d), full(1, d), full(d, w.shape[1]), full(d, ROPE_COLS), tok(128), tok(128)],
        out_specs=[tok(a), tok(PROJ_KV), tok(SSM_WIDTH), tok(GATE_PAD)],
        out_shape=[jax.ShapeDtypeStruct((s, a), bf16), jax.ShapeDtypeStruct((s, PROJ_KV), bf16),
                   jax.ShapeDtypeStruct((s, SSM_WIDTH), f32), jax.ShapeDtypeStruct((s, GATE_PAD), f32)],
        compiler_params=_params("arbitrary"),
        name="in_proj",
    )(x, g.reshape(1, d), w.astype(bf16), wp.astype(bf16), cos2, sin2)


def _compress_kernel(rk_ref, rv_ref, pek_ref, pev_ref, w1k_ref, w1v_ref, w2k_ref, w2vT_ref, cos_ref, sin_ref,
                     swap_ref, kc_ref, vcT_ref):
    f32, bf16 = jnp.float32, jnp.bfloat16
    ncp = rk_ref.shape[1]
    half = CMP_STRIDE * HEAD_DIM
    rk = rk_ref[0]
    first = jnp.dot(rk, w1k_ref[:half, :], preferred_element_type=f32)
    second = jnp.dot(rk, w1k_ref[half:, :], preferred_element_type=f32)
    pe_term = jnp.dot(pek_ref[...], w1k_ref[...], preferred_element_type=f32)
    hid = jax.nn.gelu(first + pltpu.roll(second, ncp - 1, axis=0) + pe_term)
    kc = jnp.dot(hid.astype(bf16), w2k_ref[...], preferred_element_type=f32)
    partner = jnp.dot(kc, swap_ref[...], precision=HI, preferred_element_type=f32)
    kc_ref[0] = (kc * cos_ref[...] + partner * sin_ref[...]).astype(bf16)
    rv = rv_ref[0]
    nt = (((1,), (1,)), ((), ()))
    first_t = lax.dot_general(w1v_ref[:, :half], rv, nt, preferred_element_type=f32)
    second_t = lax.dot_general(w1v_ref[:, half:], rv, nt, preferred_element_type=f32)
    pe_t = jnp.sum(w1v_ref[...].astype(f32) * pev_ref[...].astype(f32), axis=1, keepdims=True)
    hid_t = jax.nn.gelu(first_t + pltpu.roll(second_t, ncp - 1, axis=1) + pe_t)
    vcT_ref[0] = jnp.dot(w2vT_ref[...], hid_t.astype(bf16), preferred_element_type=f32).astype(bf16)


def compress_pallas(kc_raw, vc_raw, pe_k, w1k, w2k, pe_v, w1v, w2v):
    f32, bf16 = jnp.float32, jnp.bfloat16
    s = kc_raw.shape[0]
    ncp = s // CMP_STRIDE
    wide = CMP_STRIDE * HEAD_DIM
    regroup = lambda v: v.reshape(ncp, CMP_STRIDE, N_KV, HEAD_DIM).transpose(2, 0, 1, 3).reshape(N_KV, ncp, wide)
    cos, sin = _rope_tables(jnp.arange(ncp) * CMP_STRIDE + CMP_LEN - 1, HEAD_DIM)
    swap = _rope_partner(jnp.eye(HEAD_DIM, dtype=f32))
    per_g = lambda g: (g, 0, 0)
    full = lambda r, n: pl.BlockSpec((r, n), lambda g: (0, 0))
    return pl.pallas_call(
        _compress_kernel,
        grid=(N_KV,),
        in_specs=[pl.BlockSpec((1, ncp, wide), per_g), pl.BlockSpec((1, ncp, wide), per_g),
                  full(1, 2 * wide), full(1, 2 * wide), full(2 * wide, CMP_HIDDEN), full(CMP_HIDDEN, 2 * wide),
                  full(CMP_HIDDEN, HEAD_DIM), full(HEAD_DIM, CMP_HIDDEN), full(ncp, HEAD_DIM), full(ncp, HEAD_DIM),
                  full(HEAD_DIM, HEAD_DIM)],
        out_specs=[pl.BlockSpec((1, ncp, HEAD_DIM), per_g), pl.BlockSpec((1, HEAD_DIM, ncp), per_g)],
        out_shape=[jax.ShapeDtypeStruct((N_KV, ncp, HEAD_DIM), bf16), jax.ShapeDtypeStruct((N_KV, HEAD_DIM, ncp), bf16)],
        compiler_params=_params("arbitrary"),
        name="compress",
    )(regroup(kc_raw), regroup(vc_raw), pe_k.reshape(1, 2 * wide).astype(bf16), pe_v.reshape(1, 2 * wide).astype(bf16),
      w1k.astype(bf16), w1v.T.astype(bf16), w2k.astype(bf16), w2v.T.astype(bf16), cos, sin, swap)


def _nsa_kernel(qT_ref, gate_ref, kc_ref, vcT_ref, ks_ref, vsT_ref, kw_ref, vwT_ref, o_ref,
                ps_ref, pen_ref, pick_scr, s_scr, p_scr, cm_scr, m_scr, acc_scr, *, tq, n_sel):
    f32, bf16 = jnp.float32, jnp.bfloat16
    i = pl.program_id(1)
    qT = qT_ref[0, 0]
    ncol = qT.shape[1]
    ncp = kc_ref.shape[1]
    nsb = pen_ref.shape[0]
    t0 = i * tq
    t_col = t0 + (lax.broadcasted_iota(jnp.int32, (1, ncol), 1) & (tq - 1))
    t_tok = t0 + lax.broadcasted_iota(jnp.int32, (1, tq), 1)

    s = jnp.dot(kc_ref[0], qT, preferred_element_type=f32)
    n_idx = lax.broadcasted_iota(jnp.int32, (ncp, ncol), 0)
    cmask = (n_idx * CMP_STRIDE + (CMP_LEN - 1)) <= t_col
    s = jnp.where(cmask, s, NEG_INF)
    m = jnp.max(s, axis=0, keepdims=True)
    e = jnp.where(cmask, jnp.exp2(s - m), 0.0)
    l = jnp.sum(e, axis=0, keepdims=True)
    p = e / jnp.maximum(l, 1e-30)
    o_c = jnp.dot(vcT_ref[0], p.astype(bf16), preferred_element_type=f32)

    psum = p[:, 0:tq]
    for h in range(1, HPG):
        psum = psum + p[:, h * tq:(h + 1) * tq]
    ps_ref[...] = psum
    parts = [ps_ref[pl.ds(k, nsb, stride=4), :] for k in range(4)]
    m_idx = lax.broadcasted_iota(jnp.int32, (nsb, tq), 0)
    prev = jnp.where(m_idx == 0, 0.0, pltpu.roll(parts[3], 1, axis=0))
    imp = parts[0] + parts[1] + parts[2] + parts[3] + prev

    cur = t_tok // SEL_BLOCK
    forced = (m_idx == 0) | (m_idx == cur) | (m_idx == cur - 1)
    valid = m_idx <= cur
    val0 = jnp.where(forced, FORCE_SCORE, jnp.where(valid, imp, NEG_INF))
    m_f = m_idx.astype(f32)

    def pick(a, val):
        mx = jnp.max(val, axis=0, keepdims=True)
        first = jnp.min(jnp.where(val == mx, m_f, 1e9), axis=0, keepdims=True)
        pick_scr[pl.ds(a, 1), :] = first
        return jnp.where(m_f == first, -BIG, val)

    lax.fori_loop(0, n_sel, pick, val0)
    pen = jnp.full((nsb, tq), -MASK_BIAS, f32)
    for a in range(n_sel):
        pen = jnp.where(m_f == pick_scr[a:a + 1, :], 0.0, pen)
    pen_ref[...] = jnp.where(valid, pen, -MASK_BIAS)

    def sweep(k, vT, q_op, mask, carry):
        m_i, acc = carry
        sj = jnp.dot(k, q_op, preferred_element_type=f32)
        if mask is not None:
            sj = jnp.where(mask, sj, NEG_INF)
        m_new = jnp.maximum(m_i, jnp.max(sj, axis=0, keepdims=True))
        alpha = jnp.exp2(m_i - m_new)
        pj = jnp.exp2(sj - m_new).astype(bf16)
        return m_new, alpha * acc + jnp.dot(vT, pj, preferred_element_type=f32)

    def q_with_penalty(j):
        pen = pen_ref[pl.ds(pl.multiple_of(j * BLOCKS_PER_TILE, BLOCKS_PER_TILE), BLOCKS_PER_TILE), :]
        extra = jnp.concatenate([jnp.concatenate([pen] * HPG, axis=1),
                                 jnp.zeros((AUG - BLOCKS_PER_TILE, ncol), f32)], axis=0)
        return jnp.concatenate([qT, extra.astype(bf16)], axis=0)

    def kpos_of(j):
        return j * KV_TILE + lax.broadcasted_iota(jnp.int32, (KV_TILE, ncol), 0)

    init = (jnp.full((1, ncol), NEG_INF, f32), jnp.zeros((K_AUG, ncol), f32))
    jd = t0 // KV_TILE

    def issue_scores(j, slot):
        sj = jnp.dot(ks_ref[0, j], q_with_penalty(j), preferred_element_type=f32)
        s_scr[slot] = sj
        cm_scr[slot] = jnp.max(sj, axis=0, keepdims=True)

    def stage(j, slot):
        pv = jnp.dot(vsT_ref[0, jnp.maximum(j - 1, 0)], p_scr[1 - slot], preferred_element_type=f32)
        issue_scores(j + 1, 1 - slot)
        m_i = m_scr[...]
        m_new = jnp.maximum(m_i, cm_scr[slot])
        p_scr[slot] = jnp.exp2(s_scr[slot] - m_new).astype(bf16)
        acc_scr[...] = jnp.exp2(m_i - m_new) * (acc_scr[...] + pv)
        m_scr[...] = m_new

    m_scr[...] = init[0]
    acc_scr[...] = init[1]
    p_scr[1] = jnp.zeros((KV_TILE, ncol), bf16)
    j0 = jd % 2

    @pl.when(j0 == 1)
    def _():
        m_new, acc = sweep(ks_ref[0, 0], vsT_ref[0, 0], q_with_penalty(0), None, init)
        m_scr[...] = m_new
        acc_scr[...] = acc

    issue_scores(j0, 0)

    def stage_pair(jj, _):
        j = j0 + 2 * jj
        stage(j, 0)
        stage(j + 1, 1)
        return 0

    lax.fori_loop(0, (jd - j0) // 2, stage_pair, 0)
    pv = jnp.dot(vsT_ref[0, jnp.maximum(jd - 1, 0)], p_scr[1], preferred_element_type=f32)
    sd = jnp.where(kpos_of(jd) <= t_col, s_scr[0], NEG_INF)
    m_i = m_scr[...]
    m_new = jnp.maximum(m_i, jnp.max(sd, axis=0, keepdims=True))
    acc_s = (jnp.exp2(m_i - m_new) * (acc_scr[...] + pv)
             + jnp.dot(vsT_ref[0, jd], jnp.exp2(sd - m_new).astype(bf16), preferred_element_type=f32))

    w0 = jnp.maximum(t0 - WINDOW, 0) // tq
    k_win = jnp.concatenate([kw_ref[0, w0 + c] for c in range(WIN_TILES)], axis=0)
    vT_win = jnp.concatenate([vwT_ref[0, w0 + c] for c in range(WIN_TILES)], axis=1)
    sw = jnp.dot(k_win, qT, preferred_element_type=f32)
    kpos = w0 * tq + lax.broadcasted_iota(jnp.int32, sw.shape, 0)
    wmask = (kpos <= t_col) & (t_col - kpos < WINDOW)
    sw = jnp.where(wmask, sw, NEG_INF)
    pw = jnp.where(wmask, jnp.exp2(sw - jnp.max(sw, axis=0, keepdims=True)), 0.0)
    acc_w = jnp.dot(vT_win, pw.astype(bf16), preferred_element_type=f32)

    norm = lambda acc: acc[:HEAD_DIM] / jnp.maximum(acc[HEAD_DIM:HEAD_DIM + 1], 1e-30)
    g = jax.nn.sigmoid(gate_ref[0, 0])
    o_ref[0, 0] = g[0:1] * o_c + g[1:2] * norm(acc_s) + g[2:3] * norm(acc_w)


def nsa_pallas(q, kv, gates, kc, vcT):
    bf16 = jnp.bfloat16
    s = q.shape[0]
    tq, k = Q_BLOCK, KV_WIDTH
    nq, nt = s // tq, s // KV_TILE
    ncol = HPG * tq
    ncp = kc.shape[1]
    nsb = s // SEL_BLOCK
    n_sel = min(N_SEL, nsb)
    heads = lambda v: v.reshape(s, N_KV, HEAD_DIM)
    blk = (jnp.arange(s) // SEL_BLOCK) % BLOCKS_PER_TILE
    one_hot = (blk[:, None] == jnp.arange(AUG)[None, :]).astype(bf16)
    ones_col = (jnp.arange(AUG) == 0).astype(bf16)
    widen = lambda v, extra: jnp.concatenate([heads(v), jnp.broadcast_to(extra, (s, N_KV, AUG))], axis=2)
    ks = widen(kv[:, 2 * k:3 * k], one_hot[:, None, :]).reshape(nt, KV_TILE, N_KV, K_AUG).transpose(2, 0, 1, 3)
    kw = heads(kv[:, 4 * k:5 * k]).reshape(nq, tq, N_KV, HEAD_DIM).transpose(2, 0, 1, 3)
    tile_vT = lambda v, n, w: widen(v, ones_col[None, None, :]).reshape(n, w, N_KV, K_AUG).transpose(2, 0, 3, 1)
    qT = q.reshape(nq, tq, N_KV, HPG, HEAD_DIM).transpose(2, 0, 4, 3, 1).reshape(N_KV, nq, HEAD_DIM, ncol)
    gateT = gates[:, :3 * N_HEADS].reshape(nq, tq, N_KV, HPG, 3).transpose(2, 0, 4, 3, 1).reshape(N_KV, nq, 3, ncol)
    per_q = lambda g, i: (g, i, 0, 0)
    per_g3 = lambda g, i: (g, 0, 0)
    per_g4 = lambda g, i: (g, 0, 0, 0)
    oT = pl.pallas_call(
        functools.partial(_nsa_kernel, tq=tq, n_sel=n_sel),
        grid=(N_KV, nq),
        in_specs=[pl.BlockSpec((1, 1, HEAD_DIM, ncol), per_q),
                  pl.BlockSpec((1, 1, 3, ncol), per_q),
                  pl.BlockSpec((1, ncp, HEAD_DIM), per_g3),
                  pl.BlockSpec((1, HEAD_DIM, ncp), per_g3),
                  pl.BlockSpec((1, nt, KV_TILE, K_AUG), per_g4),
                  pl.BlockSpec((1, nt, K_AUG, KV_TILE), per_g4),
                  pl.BlockSpec((1, nq, tq, HEAD_DIM), per_g4),
                  pl.BlockSpec((1, nq, K_AUG, tq), per_g4)],
        out_specs=pl.BlockSpec((1, 1, HEAD_DIM, ncol), per_q),
        out_shape=jax.ShapeDtypeStruct((N_KV, nq, HEAD_DIM, ncol), jnp.float32),
        scratch_shapes=[pltpu.VMEM((ncp, tq), jnp.float32), pltpu.VMEM((nsb, tq), jnp.float32),
                        pltpu.VMEM((n_sel, tq), jnp.float32),
                        pltpu.VMEM((2, KV_TILE, ncol), jnp.float32), pltpu.VMEM((2, KV_TILE, ncol), bf16),
                        pltpu.VMEM((2, 1, ncol), jnp.float32), pltpu.VMEM((1, ncol), jnp.float32),
                        pltpu.VMEM((K_AUG, ncol), jnp.float32)],
        compiler_params=_params("arbitrary", "arbitrary"),
        name="nsa_attention",
    )(qT, gateT, kc, vcT, ks, tile_vT(kv[:, 3 * k:4 * k], nt, KV_TILE), kw, tile_vT(kv[:, 5 * k:6 * k], nq, tq))
    return oT.reshape(N_KV, nq, HEAD_DIM, HPG, tq).transpose(1, 4, 0, 3, 2).reshape(s, ATTN_WIDTH)


def _cplx_pow(lr_dt, li_dt, n):
    mag = jnp.exp(n * lr_dt)
    return mag * jnp.cos(n * li_dt), mag * jnp.sin(n * li_dt)


def _ssm_kernel(u_ref, ldt_ref, lr_row_ref, li_row_ref, lr_col_ref, li_col_ref,
                bre_t_ref, bim_t_ref, cre_t_ref, cim_t_ref, y_ref):
    f32, bf16 = jnp.float32, jnp.bfloat16
    L, C, P = SSM_CHUNK, SSM_GROUP, SSM_STATE
    nc = u_ref.shape[1]
    dt = jnp.exp(ldt_ref[0])

    lr_r, li_r = lr_row_ref[0], li_row_ref[0]
    ar, ai = _cplx_pow(lr_r * dt, li_r * dt, 1.0)
    den = lr_r * lr_r + li_r * li_r
    fr_r = ((ar - 1.0) * lr_r + ai * li_r) / den
    fi_r = (ai * lr_r - (ar - 1.0) * li_r) / den
    bbt_re = fr_r * bre_t_ref[0] - fi_r * bim_t_ref[0]
    bbt_im = fr_r * bim_t_ref[0] + fi_r * bre_t_ref[0]
    lr_c, li_c = lr_col_ref[0], li_col_ref[0]
    lane = lax.broadcasted_iota(jnp.int32, (1, L * C), 1)
    step = (lane // C).astype(f32)
    tile = (lax.broadcasted_iota(jnp.int32, (C, L * C), 0) == (lane % C)).astype(f32)
    cmt_re = jnp.dot(cre_t_ref[0], tile, precision=HI, preferred_element_type=f32)
    cmt_im = jnp.dot(cim_t_ref[0], tile, precision=HI, preferred_element_type=f32)

    def c_times_pow(n):
        pr, pi = _cplx_pow(lr_c * dt, li_c * dt, n)
        return pr * cmt_re - pi * cmt_im, pr * cmt_im + pi * cmt_re

    w_re, w_im = c_times_pow(step)
    taps = (jnp.dot(bbt_re, w_re, precision=HI, preferred_element_type=f32)
            - jnp.dot(bbt_im, w_im, precision=HI, preferred_element_type=f32))
    slabs = []
    for r in range(L):
        sl = taps if r == 0 else jnp.where(lane >= r * C, pltpu.roll(taps, r * C, axis=1), 0.0)
        slabs.append(sl)
    t_mat = jnp.concatenate(slabs, axis=0).astype(bf16)

    row_step = (lax.broadcasted_iota(jnp.int32, (L * C, 1), 0) // C).astype(f32)
    pr, pi = _cplx_pow(lr_r * dt, li_r * dt, (L - 1.0) - row_step)
    bt_re = jnp.concatenate([bbt_re] * L, axis=0)
    bt_im = jnp.concatenate([bbt_im] * L, axis=0)
    m_re = (pr * bt_re - pi * bt_im).astype(bf16)
    m_im = (pr * bt_im + pi * bt_re).astype(bf16)
    n_re, n_im = c_times_pow(step + 1.0)

    u = u_ref[0]
    e_re = jnp.dot(u, m_re, preferred_element_type=f32)
    e_im = jnp.dot(u, m_im, preferred_element_type=f32)
    k_idx = lax.broadcasted_iota(jnp.int32, (nc, P), 0)
    qr, qi = _cplx_pow(lr_r * dt, li_r * dt, float(L))
    d = 1
    while d < nc:
        sr = jnp.where(k_idx >= d, pltpu.roll(e_re, d, axis=0), 0.0)
        si = jnp.where(k_idx >= d, pltpu.roll(e_im, d, axis=0), 0.0)
        e_re, e_im = e_re + qr * sr - qi * si, e_im + qr * si + qi * sr
        qr, qi = qr * qr - qi * qi, 2.0 * qr * qi
        d *= 2
    x_re = jnp.where(k_idx >= 1, pltpu.roll(e_re, 1, axis=0), 0.0)
    x_im = jnp.where(k_idx >= 1, pltpu.roll(e_im, 1, axis=0), 0.0)
    y = jnp.dot(u, t_mat, preferred_element_type=f32)
    y = y + jnp.dot(x_re.astype(bf16), n_re.astype(bf16), preferred_element_type=f32)
    y = y - jnp.dot(x_im.astype(bf16), n_im.astype(bf16), preferred_element_type=f32)
    y_ref[0] = y


def ssm_scan_pallas(u, lam_re, lam_im, log_dt, b_re, b_im, c_re, c_im):
    f32, bf16 = jnp.float32, jnp.bfloat16
    s = u.shape[0]
    G, C, P, L = SSM_GROUPS, SSM_GROUP, SSM_STATE, SSM_CHUNK
    nc = s // L
    ug = u.reshape(nc, L, G, C).transpose(2, 0, 1, 3).reshape(G, nc, L * C).astype(bf16)
    per_g = lambda g: (g, 0, 0)
    spec = lambda a, b: pl.BlockSpec((1, a, b), per_g)
    y = pl.pallas_call(
        _ssm_kernel,
        grid=(G,),
        in_specs=[spec(nc, L * C), spec(1, 1), spec(1, P), spec(1, P), spec(P, 1), spec(P, 1),
                  spec(C, P), spec(C, P), spec(P, C), spec(P, C)],
        out_specs=spec(nc, L * C),
        out_shape=jax.ShapeDtypeStruct((G, nc, L * C), f32),
        compiler_params=_params("arbitrary"),
        name="ssm_scan",
    )(ug, log_dt.reshape(G, 1, 1), lam_re.reshape(G, 1, P), lam_im.reshape(G, 1, P),
      lam_re.reshape(G, P, 1), lam_im.reshape(G, P, 1),
      b_re.transpose(0, 2, 1), b_im.transpose(0, 2, 1), c_re.transpose(0, 2, 1), c_im.transpose(0, 2, 1))
    return y.reshape(G, nc, L, C).transpose(1, 2, 0, 3).reshape(s, G * C)


def _mix_kernel(h_ref, a_ref, y_ref, u_ref, d_ref, wglu_ref, wout_ref, o_ref):
    f32, bf16 = jnp.float32, jnp.bfloat16
    y = jax.nn.gelu(y_ref[...] + d_ref[...] * u_ref[...]).astype(bf16)
    ab = jnp.dot(y, wglu_ref[...], preferred_element_type=f32)
    s_out = ab[:, :SSM_WIDTH] * jax.nn.sigmoid(ab[:, SSM_WIDTH:])
    mix = jnp.concatenate([a_ref[...].astype(bf16), s_out.astype(bf16)], axis=1)
    o_ref[...] = h_ref[...] + jnp.dot(mix, wout_ref[...], preferred_element_type=f32)


def mix_pallas(h, a_out, y, u, d_skip, w_glu, w_out):
    bf16 = jnp.bfloat16
    s, d = h.shape
    tt = min(TOK_TILE, s)
    tok = lambda n: pl.BlockSpec((tt, n), lambda t: (t, 0))
    full = lambda r, n: pl.BlockSpec((r, n), lambda t: (0, 0))
    return pl.pallas_call(
        _mix_kernel,
        grid=(s // tt,),
        in_specs=[tok(d), tok(ATTN_WIDTH), tok(SSM_WIDTH), tok(SSM_WIDTH), full(1, SSM_WIDTH),
                  full(SSM_WIDTH, 2 * SSM_WIDTH), full(MIX_WIDTH, d)],
        out_specs=tok(d),
        out_shape=jax.ShapeDtypeStruct((s, d), jnp.float32),
        compiler_params=_params("arbitrary"),
        name="mix_out",
    )(h, a_out, y, u, d_skip.reshape(1, SSM_WIDTH), w_glu.astype(bf16), w_out.astype(bf16))


def _ranked_extract(val, k, v_scr, i_scr, off):
    r_f = lax.broadcasted_iota(jnp.int32, val.shape, 0).astype(jnp.float32)

    def body(a, v):
        mx = jnp.max(v, axis=0, keepdims=True)
        first = jnp.min(jnp.where(v == mx, r_f, 1e9), axis=0, keepdims=True)
        v_scr[pl.ds(off + a, 1), :] = mx
        i_scr[pl.ds(off + a, 1), :] = first
        return jnp.where(r_f == first, -BIG, v)

    lax.fori_loop(0, k, body, val)


def _round_of_row(shape, k, i_scr, off):
    r_f = lax.broadcasted_iota(jnp.int32, shape, 0).astype(jnp.float32)
    rank = jnp.full(shape, NO_RANK, jnp.float32)
    for a in range(k):
        rank = jnp.where(r_f == i_scr[off + a:off + a + 1, :], float(a), rank)
    return rank


def _peer_score_kernel(h_ref, g_ref, wqT_ref, sub1_ref, sub2_ref, xnT_ref, r1_ref, e1_ref, rr_ref, e2_ref,
                       q_scr, v_scr, i_scr):
    f32, bf16 = jnp.float32, jnp.bfloat16
    k = PEER_TOPK
    x = h_ref[...]
    xn = x * lax.rsqrt(jnp.mean(x * x, axis=-1, keepdims=True) + NORM_EPS) * g_ref[...]
    xnT_ref[...] = xn.T.astype(bf16)
    q_scr[...] = lax.dot_general(wqT_ref[...], xn.astype(bf16), (((1,), (1,)), ((), ())),
                                 preferred_element_type=f32)
    chunk = v_scr.shape[1]

    def head_chunk(h, lanes):
        base = pl.multiple_of(h * PEER_QDIM, PEER_QDIM)
        s1 = jnp.dot(sub1_ref[...], q_scr[pl.ds(base, PEER_HALF), lanes].astype(bf16), preferred_element_type=f32)
        s2 = jnp.dot(sub2_ref[...], q_scr[pl.ds(base + PEER_HALF, PEER_HALF), lanes].astype(bf16),
                     preferred_element_type=f32)
        _ranked_extract(s1, k, v_scr, i_scr, 0)
        _ranked_extract(s2, k, v_scr, i_scr, k)
        v1 = v_scr[0:k, :]
        v2 = v_scr[k:2 * k, :]
        cand = jnp.concatenate([v1[0:1] + v2] + [v1[a:a + 1] + v2[0:8] for a in range(1, 8)] + [v1[8:k] + v2[0:1]],
                               axis=0)
        _ranked_extract(cand, k, v_scr, i_scr, 2 * k)
        picked = (_round_of_row(cand.shape, k, i_scr, 2 * k) < NO_RANK).astype(f32)
        m1, m2 = v1[0:1], v2[0:1]
        z = jnp.sum(picked * jnp.exp(cand - (m1 + m2)), axis=0, keepdims=True)
        low = picked[0:8]
        for a in range(1, 8):
            low = low + picked[8 + 8 * a:16 + 8 * a]
        row0 = lax.broadcasted_iota(jnp.int32, low.shape, 0) == 0
        low = low + jnp.where(row0, jnp.sum(picked[72:80], axis=0, keepdims=True), 0.0)
        counts = jnp.concatenate([low, picked[8:16]], axis=0)
        r_f = lax.broadcasted_iota(jnp.int32, s2.shape, 0).astype(f32)
        rr = jnp.zeros_like(s2)
        for b in range(k):
            rr = jnp.where(r_f == i_scr[k + b:k + b + 1, :], counts[b:b + 1], rr)
        r1_ref[h, :, lanes] = _round_of_row(s1.shape, k, i_scr, 0)
        e1_ref[h, :, lanes] = jnp.exp(s1 - m1)
        rr_ref[h, :, lanes] = rr.astype(bf16)
        e2_ref[h, :, lanes] = (jnp.exp(s2 - m2) / z).astype(bf16)

    def head(h, _):
        for c in range(x.shape[0] // chunk):
            head_chunk(h, slice(c * chunk, (c + 1) * chunk))
        return 0

    lax.fori_loop(0, PEER_HEADS, head, 0)


def _peer_expert_kernel(xnT_ref, r1_ref, e1_ref, rr_ref, e2_ref, u_ref, vT_ref, o_ref, a_scr, *, n_blocks):
    f32, bf16 = jnp.float32, jnp.bfloat16
    e = pl.program_id(1)
    half = u_ref.shape[0] // 2
    n_i = half // PEER_NKEYS

    @pl.when(e == 0)
    def _():
        o_ref[...] = jnp.zeros_like(o_ref)
        a_scr[...] = jnp.zeros_like(a_scr)

    i0 = jnp.minimum(e, n_blocks - 1) * (2 * n_i)

    def spread(row):
        tile = jnp.broadcast_to(row, (16, row.shape[1])).astype(bf16)
        return jnp.concatenate([tile] * (PEER_NKEYS // 16), axis=0)

    def gate(i):
        w = None
        for h in range(PEER_HEADS):
            r1_row = spread(r1_ref[h, pl.ds(i, 1), :])
            e1_row = spread(e1_ref[h, pl.ds(i, 1), :])
            term = jnp.where(r1_row < rr_ref[h], e2_ref[h], jnp.zeros((), bf16)) * e1_row
            w = term if w is None else w + term
        return w

    act = jax.nn.gelu(jnp.dot(u_ref[...], xnT_ref[...], preferred_element_type=f32).astype(bf16))
    rows = lambda k: slice(k * PEER_NKEYS, (k + 1) * PEER_NKEYS)
    for ii in range(n_i):
        a_scr[half + ii * PEER_NKEYS:half + (ii + 1) * PEER_NKEYS, :] = act[rows(ii)] * gate(i0 + ii)
    o_ref[...] += jnp.dot(vT_ref[...], a_scr[...], preferred_element_type=f32)
    for ii in range(n_i):
        a_scr[rows(ii), :] = act[rows(n_i + ii)] * gate(i0 + n_i + ii)


def peer_ffn_pallas(h1, g, w_q, sub1, sub2, u_tab, v_tab):
    f32, bf16 = jnp.float32, jnp.bfloat16
    s, d = h1.shape
    n_exp = u_tab.shape[0]
    tt = min(TOK_TILE, s)
    nq = PEER_HEADS * PEER_QDIM
    keys = lambda dt: jax.ShapeDtypeStruct((PEER_HEADS, PEER_NKEYS, s), dt)
    key_spec1 = pl.BlockSpec((PEER_HEADS, PEER_NKEYS, tt), lambda t: (0, 0, t))
    xnT, r1, e1, rr, e2 = pl.pallas_call(
        _peer_score_kernel,
        grid=(s // tt,),
        in_specs=[pl.BlockSpec((tt, d), lambda t: (t, 0)),
                  pl.BlockSpec((1, d), lambda t: (0, 0)),
                  pl.BlockSpec((nq, d), lambda t: (0, 0)),
                  pl.BlockSpec((PEER_NKEYS, PEER_HALF), lambda t: (0, 0)),
                  pl.BlockSpec((PEER_NKEYS, PEER_HALF), lambda t: (0, 0))],
        out_specs=[pl.BlockSpec((d, tt), lambda t: (0, t)), key_spec1, key_spec1, key_spec1, key_spec1],
        out_shape=[jax.ShapeDtypeStruct((d, s), bf16), keys(f32), keys(f32), keys(bf16), keys(bf16)],
        scratch_shapes=[pltpu.VMEM((nq, tt), f32), pltpu.VMEM((3 * PEER_TOPK, min(PEER_LANE_CHUNK, tt)), f32),
                        pltpu.VMEM((3 * PEER_TOPK, min(PEER_LANE_CHUNK, tt)), f32)],
        compiler_params=_params("arbitrary"),
        name="peer_scores",
    )(h1, g.reshape(1, d), w_q.T.astype(bf16), sub1.astype(bf16), sub2.astype(bf16))
    te = PEER_EXP_TILE
    half = te // 2
    n_blocks = n_exp // te
    key_spec2 = pl.BlockSpec((PEER_HEADS, PEER_NKEYS, tt), lambda t, e: (0, 0, t))
    pad = jnp.zeros((d, half), bf16)
    v_t = jnp.concatenate([pad, v_tab.T.astype(bf16), pad], axis=1)
    return pl.pallas_call(
        functools.partial(_peer_expert_kernel, n_blocks=n_blocks),
        grid=(s // tt, n_blocks + 1),
        in_specs=[pl.BlockSpec((d, tt), lambda t, e: (0, t)), key_spec2, key_spec2, key_spec2, key_spec2,
                  pl.BlockSpec((te, d), lambda t, e: (jnp.minimum(e, n_blocks - 1), 0)),
                  pl.BlockSpec((d, te), lambda t, e: (0, e))],
        out_specs=pl.BlockSpec((d, tt), lambda t, e: (0, t)),
        out_shape=jax.ShapeDtypeStruct((d, s), f32),
        scratch_shapes=[pltpu.VMEM((te, tt), bf16)],
        compiler_params=_params("arbitrary", "arbitrary"),
        name="peer_experts",
    )(xnT, r1, e1, rr, e2, u_tab.astype(bf16), v_t)


def _final_kernel(h_ref, pT_ref, g_ref, o_ref):
    xf = h_ref[...] + pT_ref[...].T
    y = xf * lax.rsqrt(jnp.mean(xf * xf, axis=-1, keepdims=True) + NORM_EPS)
    o_ref[...] = y * g_ref[...]


def final_pallas(h, peer_t, g):
    s, d = h.shape
    tt = min(TOK_TILE, s)
    return pl.pallas_call(
        _final_kernel,
        grid=(s // tt,),
        in_specs=[pl.BlockSpec((tt, d), lambda t: (t, 0)), pl.BlockSpec((d, tt), lambda t: (0, t)),
                  pl.BlockSpec((1, d), lambda t: (0, 0))],
        out_specs=pl.BlockSpec((tt, d), lambda t: (t, 0)),
        out_shape=jax.ShapeDtypeStruct((s, d), jnp.float32),
        compiler_params=_params("arbitrary"),
        name="final_norm",
    )(h, peer_t, g.reshape(1, d))


def kernel(x, attn_norm, w_in, cmp_k_pe, cmp_k_w1, cmp_k_w2, cmp_v_pe, cmp_v_w1, cmp_v_w2, ssm_lam_re, ssm_lam_im, ssm_log_dt, ssm_b_re, ssm_b_im, ssm_c_re, ssm_c_im, ssm_d, ssm_w_glu, w_out, ffn_norm, peer_w_q, peer_subkeys_1, peer_subkeys_2, peer_u, peer_v, final_norm):
    b, s, d = x.shape
    assert b == 1 and attn_norm.shape[0] == 1 and s % max(TOK_TILE, KV_TILE) == 0 and s >= WIN_TILES * Q_BLOCK
    h = x[0]
    q, kv, u, gates = in_proj_pallas(h, attn_norm[0], w_in[0])
    kc, vcT = compress_pallas(kv[:, :KV_WIDTH], kv[:, KV_WIDTH:2 * KV_WIDTH], cmp_k_pe[0], cmp_k_w1[0], cmp_k_w2[0],
                              cmp_v_pe[0], cmp_v_w1[0], cmp_v_w2[0])
    a_out = nsa_pallas(q, kv, gates, kc, vcT)
    y = ssm_scan_pallas(u, ssm_lam_re[0], ssm_lam_im[0], ssm_log_dt[0], ssm_b_re[0], ssm_b_im[0],
                        ssm_c_re[0], ssm_c_im[0])
    h1 = mix_pallas(h, a_out, y, u, ssm_d[0], ssm_w_glu[0], w_out[0])
    peer_t = peer_ffn_pallas(h1, ffn_norm[0], peer_w_q[0], peer_subkeys_1[0], peer_subkeys_2[0], peer_u[0], peer_v[0])
    return final_pallas(h1, peer_t, final_norm)[None]
```
